```python
import math
import jax, jax.numpy as jnp
from jax import lax
import numpy as np

D_MODEL = 1024
BATCH = 4
SEQ = 4096
DEPTH = 1

MEM_LEN = 256
C_CONV = D_MODEL
CONV_WIDTH = 31
SB_HEADS = 16
SB_HEAD_DIM = D_MODEL // SB_HEADS
D_ATT = SB_HEADS * SB_HEAD_DIM
N_BRANCHES = 2
D_IN = 2 * C_CONV + 3 * D_ATT + N_BRANCHES * D_MODEL
Q_BLOCK = 128
XA_HEADS = 4
XA_HEAD_DIM = D_MODEL // XA_HEADS
PEER_HEADS = 8
PEER_KEYS = 128
PEER_EXPERTS = PEER_KEYS * PEER_KEYS
PEER_QDIM = 256
PEER_HALF = PEER_QDIM // 2
PEER_TOPK = 16
TOK_BLOCK = 128
EPS = 1e-6

kernel_name = "hybrid_conv_stickbreak_peer_block"


def rms_norm(x, g):
    xf = x.astype(jnp.float32)
    xf = xf * lax.rsqrt(jnp.mean(xf * xf, axis=-1, keepdims=True) + EPS)
    return (xf * g.astype(jnp.float32)).astype(x.dtype)


def layer_norm(x, g, b):
    xf = x.astype(jnp.float32)
    mu = jnp.mean(xf, axis=-1, keepdims=True)
    var = jnp.mean(jnp.square(xf - mu), axis=-1, keepdims=True)
    y = (xf - mu) * lax.rsqrt(var + EPS)
    return (y * g.astype(jnp.float32) + b.astype(jnp.float32)).astype(x.dtype)


def conformer_conv(glu_in, dw_w, dw_b, ln_g, ln_b, w_proj):
    a, gt = jnp.split(glu_in, 2, axis=-1)
    u = a * jax.nn.sigmoid(gt)
    y = lax.conv_general_dilated(
        u, dw_w[:, None, :], window_strides=(1,),
        padding=[(CONV_WIDTH - 1, 0)],
        dimension_numbers=('NWC', 'WIO', 'NWC'),
        feature_group_count=C_CONV) + dw_b
    y = jax.nn.silu(layer_norm(y, ln_g, ln_b))
    return y @ w_proj


def stick_breaking_attention(q, k, v):
    b, s, h, dh = q.shape
    n_blk = s // Q_BLOCK
    kt = k.transpose(0, 2, 1, 3)
    vt = v.transpose(0, 2, 1, 3)
    qb = q.reshape(b, n_blk, Q_BLOCK, h, dh).transpose(1, 0, 3, 2, 4)
    starts = jnp.arange(n_blk, dtype=jnp.int32) * Q_BLOCK
    key_pos = jnp.arange(s, dtype=jnp.int32)
    scale = 1.0 / math.sqrt(dh)

    def block(args):
        q_blk, t0 = args
        z = jnp.einsum('bhqd,bhkd->bhqk', q_blk, kt).astype(jnp.float32) * scale
        q_pos = t0 + jnp.arange(Q_BLOCK, dtype=jnp.int32)
        causal = key_pos[None, :] < q_pos[:, None]
        log_1m = jnp.where(causal, -jax.nn.softplus(z), 0.0)
        after = lax.cumsum(log_1m, axis=3, reverse=True) - log_1m
        w = jnp.where(causal, jnp.exp(jax.nn.log_sigmoid(z) + after), 0.0)
        return jnp.einsum('bhqk,bhkd->bhqd', w.astype(vt.dtype), vt)

    o = lax.map(block, (qb, starts))
    return o.transpose(1, 0, 3, 2, 4).reshape(b, s, h * dh)


def memory_cross_attention(h, m, w_xq, w_xkv, w_xo):
    b, s, _ = h.shape
    m_len = m.shape[1]
    q = (h @ w_xq).reshape(b, s, XA_HEADS, XA_HEAD_DIM)
    kv = (m @ w_xkv).reshape(b, m_len, 2, XA_HEADS, XA_HEAD_DIM)
    k, v = kv[:, :, 0], kv[:, :, 1]
    sc = jnp.einsum('bshd,bmhd->bhsm', q, k).astype(jnp.float32) / math.sqrt(XA_HEAD_DIM)
    p = jax.nn.softmax(sc, axis=-1).astype(v.dtype)
    o = jnp.einsum('bhsm,bmhd->bshd', p, v).reshape(b, s, D_MODEL)
    return o @ w_xo


def peer_ffn(h, w_pq, sub_keys, u_emb, v_emb):
    b, s, d = h.shape
    q = (h @ w_pq).reshape(b, s, PEER_HEADS, 2, PEER_HALF)
    sc = jnp.einsum('bshpc,pnc->bshpn', q, sub_keys).astype(jnp.float32)
    top_v, top_i = lax.top_k(sc, PEER_TOPK)
    cand = (top_v[..., 0, :, None] + top_v[..., 1, None, :]).reshape(
        b, s, PEER_HEADS, PEER_TOPK * PEER_TOPK)
    best_v, best_i = lax.top_k(cand, PEER_TOPK)
    i1 = jnp.take_along_axis(top_i[..., 0, :], best_i // PEER_TOPK, axis=-1)
    i2 = jnp.take_along_axis(top_i[..., 1, :], best_i % PEER_TOPK, axis=-1)
    expert = i1 * PEER_KEYS + i2
    gate = jax.nn.softmax(best_v, axis=-1)
    n_blk = s // TOK_BLOCK

    def to_blocks(a):
        return a.reshape(b, n_blk, TOK_BLOCK, *a.shape[2:]).swapaxes(0, 1)

    def block(args):
        h_blk, e_blk, g_blk = args
        u = u_emb[e_blk]
        a = jnp.einsum('bthkd,btd->bthk', u, h_blk).astype(jnp.float32)
        act = (jax.nn.gelu(a) * g_blk).astype(h_blk.dtype)
        return jnp.einsum('bthk,bthkd->btd', act, v_emb[e_blk])

    out = lax.map(block, (to_blocks(h), to_blocks(expert), to_blocks(gate)))
    return out.swapaxes(0, 1).reshape(b, s, d)


def setup_inputs(seed: int = 0) -> dict:
    key = jax.random.key(seed)
    ks = jax.random.split(key, 26)

    def nrm(k, shape, scale):
        return jax.random.normal(k, shape, jnp.float32) * scale

    def gain(k, shape):
        return 1.0 + 0.01 * jax.random.normal(k, shape, jnp.float32)

    L = DEPTH
    return {
        "x": nrm(ks[0], (BATCH, SEQ, D_MODEL), 1.0),
        "mem": nrm(ks[1], (BATCH, MEM_LEN, D_MODEL), 1.0),
        "norm_mix_g": gain(ks[2], (L, D_MODEL)),
        "w_in": nrm(ks[3], (L, D_MODEL, D_IN), D_MODEL ** -0.5),
        "b_gate": nrm(ks[4], (L, N_BRANCHES * D_MODEL), 0.1),
        "conv_dw_w": nrm(ks[5], (L, CONV_WIDTH, C_CONV), CONV_WIDTH ** -0.5),
        "conv_dw_b": nrm(ks[6], (L, C_CONV), 0.02),
        "conv_ln_g": gain(ks[7], (L, C_CONV)),
        "conv_ln_b": nrm(ks[8], (L, C_CONV), 0.02),
        "w_conv_out": nrm(ks[9], (L, C_CONV, D_MODEL), C_CONV ** -0.5),
        "w_attn_out": nrm(ks[10], (L, D_ATT, D_MODEL), D_ATT ** -0.5),
        "w_out": nrm(ks[11], (L, D_MODEL, D_MODEL), D_MODEL ** -0.5),
        "norm_xattn_g": gain(ks[12], (L, D_MODEL)),
        "norm_mem_g": gain(ks[13], (L, D_MODEL)),
        "w_xq": nrm(ks[14], (L, D_MODEL, D_MODEL), D_MODEL ** -0.5),
        "w_xkv": nrm(ks[15], (L, D_MODEL, 2 * D_MODEL), D_MODEL ** -0.5),
        "w_xo": nrm(ks[16], (L, D_MODEL, D_MODEL), D_MODEL ** -0.5),
        "norm_ffn_g": gain(ks[17], (L, D_MODEL)),
        "w_peer_q": nrm(ks[18], (L, D_MODEL, PEER_HEADS * PEER_QDIM), D_MODEL ** -0.5),
        "peer_sub_keys": nrm(ks[19], (L, 2, PEER_KEYS, PEER_HALF), PEER_HALF ** -0.5),
        "peer_u": nrm(ks[20], (L, PEER_EXPERTS, D_MODEL), D_MODEL ** -0.5),
        "peer_v": nrm(ks[21], (L, PEER_EXPERTS, D_MODEL), PEER_HEADS ** -0.5),
        "norm_final_g": gain(ks[22], (D_MODEL,)),
    }


def reference(x, mem, norm_mix_g, w_in, b_gate, conv_dw_w, conv_dw_b, conv_ln_g, conv_ln_b,
              w_conv_out, w_attn_out, w_out, norm_xattn_g, norm_mem_g, w_xq, w_xkv, w_xo,
              norm_ffn_g, w_peer_q, peer_sub_keys, peer_u, peer_v, norm_final_g):
    b, s, _ = x.shape
    splits = [2 * C_CONV, 2 * C_CONV + D_ATT, 2 * C_CONV + 2 * D_ATT, 2 * C_CONV + 3 * D_ATT]
    for l in range(DEPTH):
        h = rms_norm(x, norm_mix_g[l])
        proj = h @ w_in[l]
        conv_in, q, k, v, gate_logits = jnp.split(proj, splits, axis=-1)
        gates = jax.nn.sigmoid(gate_logits + b_gate[l])
        g_conv, g_attn = jnp.split(gates, 2, axis=-1)
        conv_out = conformer_conv(conv_in, conv_dw_w[l], conv_dw_b[l],
                                  conv_ln_g[l], conv_ln_b[l], w_conv_out[l])
        attn = stick_breaking_attention(q.reshape(b, s, SB_HEADS, SB_HEAD_DIM),
                                        k.reshape(b, s, SB_HEADS, SB_HEAD_DIM),
                                        v.reshape(b, s, SB_HEADS, SB_HEAD_DIM))
        attn_out = attn @ w_attn_out[l]
        x = x + (g_conv * conv_out + g_attn * attn_out) @ w_out[l]
        h = rms_norm(x, norm_xattn_g[l])
        m = rms_norm(mem, norm_mem_g[l])
        x = x + memory_cross_attention(h, m, w_xq[l], w_xkv[l], w_xo[l])
        h = rms_norm(x, norm_ffn_g[l])
        x = x + peer_ffn(h, w_peer_q[l], peer_sub_keys[l], peer_u[l], peer_v[l])
    return rms_norm(x, norm_final_g)
```

```python
import functools
import math

import jax
import jax.numpy as jnp
from jax import lax
from jax.experimental import pallas as pl
from jax.experimental.pallas import tpu as pltpu

F32 = jnp.float32
BF16 = jnp.bfloat16

D_MODEL = 1024
CONV_WIDTH = 31
SB_HEADS = 16
SB_HEAD_DIM = 64
XA_HEADS = 4
XA_HEAD_DIM = 256
PEER_HEADS = 8
PEER_KEYS = 128
PEER_TOPK = 16
EPS = 1e-6

LANES = 128
VMEM_LIMIT = 56 * 1024 * 1024

NEG_INF = float("-inf")


def _cparams(n_axes):
    return pltpu.CompilerParams(
        dimension_semantics=("arbitrary",) * n_axes, vmem_limit_bytes=VMEM_LIMIT)


def _rms(x, g):
    ms = jnp.mean(x * x, axis=-1, keepdims=True)
    return x * lax.rsqrt(ms + EPS) * g


def _norm_mm_kernel(x_ref, g_ref, w_ref, o_ref, h_scr):
    @pl.when(pl.program_id(1) == 0)
    def _():
        h_scr[...] = _rms(x_ref[...], g_ref[...]).astype(BF16)

    o_ref[...] = jnp.dot(h_scr[...], w_ref[...], preferred_element_type=F32).astype(o_ref.dtype)


def norm_matmul(x, g, w, tm, tn):
    n, d = x.shape
    m = w.shape[1]
    return pl.pallas_call(
        _norm_mm_kernel,
        grid=(n // tm, m // tn),
        in_specs=[
            pl.BlockSpec((tm, d), lambda i, j: (i, 0)),
            pl.BlockSpec((1, d), lambda i, j: (0, 0)),
            pl.BlockSpec((d, tn), lambda i, j: (0, j)),
        ],
        out_specs=pl.BlockSpec((tm, tn), lambda i, j: (i, j)),
        out_shape=jax.ShapeDtypeStruct((n, m), BF16),
        scratch_shapes=[pltpu.VMEM((tm, d), BF16)],
        compiler_params=_cparams(2),
        name="norm_matmul",
    )(x, g, w)


CONV_HALO = 32
CONV_ROWS = 64


def _dwconv_kernel(a_ref, gt_ref, ah_ref, gh_ref, w_ref, b_ref, y_ref, u_scr):
    ts = a_ref.shape[0]
    first = pl.program_id(1) == 0
    u_scr[CONV_HALO:CONV_HALO + ts, :] = (
        a_ref[...].astype(F32) * jax.nn.sigmoid(gt_ref[...].astype(F32)))
    uh = ah_ref[...].astype(F32) * jax.nn.sigmoid(gh_ref[...].astype(F32))
    u_scr[0:CONV_HALO, :] = jnp.where(first, 0.0, uh)
    off = CONV_HALO - (CONV_WIDTH - 1)
    for r in range(ts // CONV_ROWS):
        r0 = r * CONV_ROWS
        acc = jnp.zeros((CONV_ROWS, LANES), F32) + b_ref[...]
        for k in range(CONV_WIDTH):
            acc = acc + w_ref[k:k + 1, :] * u_scr[r0 + off + k:r0 + off + k + CONV_ROWS, :]
        y_ref[r0:r0 + CONV_ROWS, :] = acc


def dwconv(proj, dw_w, dw_b, batch, seq, ts):
    n = proj.shape[0]
    c = dw_w.shape[1]
    nct = c // LANES
    spt = seq // ts
    hb = ts // CONV_HALO

    def halo_idx(col0):
        def f(b, i, cc):
            return (jnp.maximum((b * spt + i) * hb - 1, 0), col0 + cc)
        return f

    return pl.pallas_call(
        _dwconv_kernel,
        grid=(batch, spt, nct),
        in_specs=[
            pl.BlockSpec((ts, LANES), lambda b, i, cc: (b * spt + i, cc)),
            pl.BlockSpec((ts, LANES), lambda b, i, cc: (b * spt + i, nct + cc)),
            pl.BlockSpec((CONV_HALO, LANES), halo_idx(0)),
            pl.BlockSpec((CONV_HALO, LANES), halo_idx(nct)),
            pl.BlockSpec((CONV_WIDTH, LANES), lambda b, i, cc: (0, cc)),
            pl.BlockSpec((1, LANES), lambda b, i, cc: (0, cc)),
        ],
        out_specs=pl.BlockSpec((ts, LANES), lambda b, i, cc: (b * spt + i, cc)),
        out_shape=jax.ShapeDtypeStruct((n, c), F32),
        scratch_shapes=[pltpu.VMEM((CONV_HALO + ts, LANES), F32)],
        compiler_params=_cparams(3),
        name="dwconv",
    )(proj, proj, proj, proj, dw_w, dw_b)


def _conv_out_kernel(y_ref, gl_ref, lng_ref, lnb_ref, bg_ref, w_ref, o_ref):
    y = y_ref[...]
    mu = jnp.mean(y, axis=-1, keepdims=True)
    yc = y - mu
    var = jnp.mean(yc * yc, axis=-1, keepdims=True)
    yn = yc * lax.rsqrt(var + EPS) * lng_ref[...] + lnb_ref[...]
    act = (yn * jax.nn.sigmoid(yn)).astype(BF16)
    co = jnp.dot(act, w_ref[...], preferred_element_type=F32)
    gate = jax.nn.sigmoid(gl_ref[...].astype(F32) + bg_ref[...])
    o_ref[...] = (gate * co).astype(o_ref.dtype)


def conv_out(y, proj, ln_g, ln_b, bg_conv, w, gate_col_block, tm):
    n, c = y.shape
    d = w.shape[1]
    return pl.pallas_call(
        _conv_out_kernel,
        grid=(n // tm,),
        in_specs=[
            pl.BlockSpec((tm, c), lambda i: (i, 0)),
            pl.BlockSpec((tm, d), lambda i: (i, gate_col_block)),
            pl.BlockSpec((1, c), lambda i: (0, 0)),
            pl.BlockSpec((1, c), lambda i: (0, 0)),
            pl.BlockSpec((1, d), lambda i: (0, 0)),
            pl.BlockSpec((c, d), lambda i: (0, 0)),
        ],
        out_specs=pl.BlockSpec((tm, d), lambda i: (i, 0)),
        out_shape=jax.ShapeDtypeStruct((n, d), BF16),
        compiler_params=_cparams(1),
        name="conv_out",
    )(y, proj, ln_g, ln_b, bg_conv, w)


SB_TQ = 256
SB_TK = 256
SB_CHUNK = 32


def _softplus(z):
    return jnp.maximum(z, 0.0) + jnp.log(1.0 + jnp.exp(-jnp.abs(z)))


def _sb_kernel(q_ref, k_ref, v_ref, tri_ref, o_ref,
               z_scr, hi_scr, lo_scr, w_scr, acc_scr, run_scr):
    qi = pl.program_id(2)
    tq, tk = SB_TQ, SB_TK
    lane = lax.broadcasted_iota(jnp.int32, (1, LANES), 1)
    q2 = q_ref[...]
    scale = 1.0 / math.sqrt(SB_HEAD_DIM)

    def scores(qh, j):
        kb = k_ref[pl.ds(pl.multiple_of(j * tk, tk), tk), :]
        z_scr[...] = lax.dot_general(qh, kb, (((1,), (1,)), ((), ())),
                                     preferred_element_type=F32)

    def cumulate():
        return (jnp.dot(hi_scr[...], tri_ref[...], preferred_element_type=F32)
                + jnp.dot(lo_scr[...], tri_ref[...], preferred_element_type=F32))

    def split_store(rows, sp):
        hi = sp.astype(BF16)
        hi_scr[rows, :] = hi
        lo_scr[rows, :] = (sp - hi.astype(F32)).astype(BF16)

    def accumulate(j):
        vb = v_ref[pl.ds(pl.multiple_of(j * tk, tk), tk), :]
        acc_scr[...] += jnp.dot(w_scr[...], vb, preferred_element_type=F32)

    def diag_block(qh):
        scores(qh, qi)
        for c in range(tq // SB_CHUNK):
            rows = slice(c * SB_CHUNK, (c + 1) * SB_CHUNK)
            z = z_scr[rows, :]
            sp = _softplus(z)
            qpos = c * SB_CHUNK + lax.broadcasted_iota(jnp.int32, (SB_CHUNK, tk), 0)
            kpos = lax.broadcasted_iota(jnp.int32, (SB_CHUNK, tk), 1)
            split_store(rows, jnp.where(kpos < qpos, sp, 0.0))
            z_scr[rows, :] = z - sp
        r = cumulate()
        for c in range(tq // SB_CHUNK):
            rows = slice(c * SB_CHUNK, (c + 1) * SB_CHUNK)
            qpos = c * SB_CHUNK + lax.broadcasted_iota(jnp.int32, (SB_CHUNK, tk), 0)
            kpos = lax.broadcasted_iota(jnp.int32, (SB_CHUNK, tk), 1)
            logw = z_scr[rows, :] + r[rows, :tk]
            w_scr[rows, :] = jnp.where(kpos < qpos, jnp.exp(logw), 0.0).astype(BF16)
        run_scr[...] = r[:, tk:]
        acc_scr[...] = jnp.zeros_like(acc_scr)
        accumulate(qi)

    def off_block(qh, j):
        scores(qh, j)
        for c in range(tq // SB_CHUNK):
            rows = slice(c * SB_CHUNK, (c + 1) * SB_CHUNK)
            z = z_scr[rows, :]
            sp = _softplus(z)
            split_store(rows, sp)
            z_scr[rows, :] = z - sp
        r = cumulate()
        for c in range(tq // SB_CHUNK):
            rows = slice(c * SB_CHUNK, (c + 1) * SB_CHUNK)
            run = run_scr[rows, :]
            logw = z_scr[rows, :] + r[rows, :tk] + jnp.concatenate([run] * (tk // LANES), axis=1)
            w_scr[rows, :] = jnp.exp(logw).astype(BF16)
        run_scr[...] += r[:, tk:]
        accumulate(j)

    outs = []
    for head in range(2):
        in_head = (lane >= head * SB_HEAD_DIM) & (lane < (head + 1) * SB_HEAD_DIM)
        qh = jnp.where(in_head, q2, jnp.zeros_like(q2)) * jnp.asarray(scale, q2.dtype)
        diag_block(qh)

        def body(jj, carry, qh=qh):
            off_block(qh, qi - 1 - jj)
            return carry

        lax.fori_loop(0, qi, body, 0)
        outs.append(acc_scr[...])
    o_ref[...] = jnp.where(lane < SB_HEAD_DIM, outs[0], outs[1]).astype(o_ref.dtype)


def _neg_tri(tk):
    j = lax.broadcasted_iota(jnp.int32, (tk, tk + LANES), 0)
    s = lax.broadcasted_iota(jnp.int32, (tk, tk + LANES), 1)
    return jnp.where((j > s) | (s >= tk), -1.0, 0.0).astype(BF16)


def stick_breaking(proj, batch, seq, q_col, k_col, v_col):
    n = proj.shape[0]
    nq = seq // SB_TQ
    npair = SB_HEADS * SB_HEAD_DIM // LANES
    tri = _neg_tri(SB_TK)
    return pl.pallas_call(
        _sb_kernel,
        grid=(batch, npair, nq),
        in_specs=[
            pl.BlockSpec((SB_TQ, LANES), lambda b, hp, qi: (b * nq + qi, q_col + hp)),
            pl.BlockSpec((seq, LANES), lambda b, hp, qi: (b, k_col + hp)),
            pl.BlockSpec((seq, LANES), lambda b, hp, qi: (b, v_col + hp)),
            pl.BlockSpec((SB_TK, SB_TK + LANES), lambda b, hp, qi: (0, 0)),
        ],
        out_specs=pl.BlockSpec((SB_TQ, LANES), lambda b, hp, qi: (b * nq + qi, hp)),
        out_shape=jax.ShapeDtypeStruct((n, SB_HEADS * SB_HEAD_DIM), BF16),
        scratch_shapes=[
            pltpu.VMEM((SB_TQ, SB_TK), F32),
            pltpu.VMEM((SB_TQ, SB_TK), BF16),
            pltpu.VMEM((SB_TQ, SB_TK), BF16),
            pltpu.VMEM((SB_TQ, SB_TK), BF16),
            pltpu.VMEM((SB_TQ, LANES), F32),
            pltpu.VMEM((SB_TQ, LANES), F32),
        ],
        compiler_params=_cparams(3),
        name="stick_breaking",
    )(proj, proj, proj, tri)


def _merge_kernel(attn_ref, gc_ref, gl_ref, x_ref, bg_ref, wa_ref, wo_ref, o_ref):
    ao = jnp.dot(attn_ref[...], wa_ref[...], preferred_element_type=F32)
    gate = jax.nn.sigmoid(gl_ref[...].astype(F32) + bg_ref[...])
    mixed = (gc_ref[...].astype(F32) + gate * ao).astype(BF16)
    o_ref[...] = x_ref[...] + jnp.dot(mixed, wo_ref[...], preferred_element_type=F32)


def merge(attn, gc, proj, x, bg_attn, w_attn_out, w_out, gate_col_block, tm):
    n, d = x.shape
    da = attn.shape[1]
    return pl.pallas_call(
        _merge_kernel,
        grid=(n // tm,),
        in_specs=[
            pl.BlockSpec((tm, da), lambda i: (i, 0)),
            pl.BlockSpec((tm, d), lambda i: (i, 0)),
            pl.BlockSpec((tm, d), lambda i: (i, gate_col_block)),
            pl.BlockSpec((tm, d), lambda i: (i, 0)),
            pl.BlockSpec((1, d), lambda i: (0, 0)),
            pl.BlockSpec((da, d), lambda i: (0, 0)),
            pl.BlockSpec((d, d), lambda i: (0, 0)),
        ],
        out_specs=pl.BlockSpec((tm, d), lambda i: (i, 0)),
        out_shape=jax.ShapeDtypeStruct((n, d), F32),
        compiler_params=_cparams(1),
        name="merge",
    )(attn, gc, proj, x, bg_attn, w_attn_out, w_out)


def _xattn_kernel(x_ref, g_ref, kv_ref, wq_ref, wo_ref, o_ref):
    x = x_ref[...]
    h = _rms(x, g_ref[...]).astype(BF16)
    q = jnp.dot(h, wq_ref[...], preferred_element_type=F32)
    q = (q * (1.0 / math.sqrt(XA_HEAD_DIM))).astype(BF16)
    d = XA_HEADS * XA_HEAD_DIM
    outs = []
    for hd in range(XA_HEADS):
        cols = slice(hd * XA_HEAD_DIM, (hd + 1) * XA_HEAD_DIM)
        kh = kv_ref[:, cols]
        vh = kv_ref[:, d + hd * XA_HEAD_DIM:d + (hd + 1) * XA_HEAD_DIM]
        sc = lax.dot_general(q[:, cols], kh, (((1,), (1,)), ((), ())),
                             preferred_element_type=F32)
        p = jnp.exp(sc - jnp.max(sc, axis=-1, keepdims=True))
        p = (p / jnp.sum(p, axis=-1, keepdims=True)).astype(BF16)
        outs.append(jnp.dot(p, vh, preferred_element_type=F32).astype(BF16))
    o = jnp.concatenate(outs, axis=1)
    o_ref[...] = x + jnp.dot(o, wo_ref[...], preferred_element_type=F32)


def cross_attention(x1, g, kv, w_xq, w_xo, seq, mem_len, tm):
    n, d = x1.shape
    tiles_per_batch = seq // tm
    return pl.pallas_call(
        _xattn_kernel,
        grid=(n // tm,),
        in_specs=[
            pl.BlockSpec((tm, d), lambda i: (i, 0)),
            pl.BlockSpec((1, d), lambda i: (0, 0)),
            pl.BlockSpec((mem_len, 2 * d), lambda i: (i // tiles_per_batch, 0)),
            pl.BlockSpec((d, d), lambda i: (0, 0)),
            pl.BlockSpec((d, d), lambda i: (0, 0)),
        ],
        out_specs=pl.BlockSpec((tm, d), lambda i: (i, 0)),
        out_shape=jax.ShapeDtypeStruct((n, d), F32),
        compiler_params=_cparams(1),
        name="cross_attention",
    )(x1, g, kv, w_xq, w_xo)


PEER_TR = 256


def _top_sorted(s, out_scr):
    cur = s
    for r in range(PEER_TOPK):
        m = jnp.max(cur, axis=0, keepdims=True)
        out_scr[r:r + 1, :] = m
        cur = jnp.where(cur >= m, NEG_INF, cur)


def _peer_route_kernel(x_ref, g_ref, wq_ref, sk_ref, h_ref, thr_ref, e1_ref, s2_ref, e2_ref,
                       q_scr, a_scr, b_scr):
    h3 = _rms(x_ref[...], g_ref[...]).astype(BF16)
    h_ref[...] = h3
    q = jnp.dot(h3, wq_ref[...], preferred_element_type=F32).astype(BF16)
    for c in range(2 * PEER_HEADS):
        q_scr[c] = q[:, c * PEER_KEYS:(c + 1) * PEER_KEYS]

    def head_body(hd, carry):
        dn = (((1,), (1,)), ((), ()))
        s1 = lax.dot_general(sk_ref[0], q_scr[2 * hd], dn, preferred_element_type=F32)
        s2 = lax.dot_general(sk_ref[1], q_scr[2 * hd + 1], dn, preferred_element_type=F32)
        _top_sorted(s1, a_scr)
        _top_sorted(s2, b_scr)
        a = a_scr[...]
        b = b_scr[...]
        blocks = [a[0:1, :] + b]
        for r in range(1, 8):
            blocks.append(a[r:r + 1, :] + b[0:8, :])
        blocks.append(a[8:16, :] + b[0:1, :])
        cur = jnp.concatenate(blocks, axis=0)
        top = a[0:1, :] + b[0:1, :]
        zsum = jnp.zeros_like(top)
        m = top
        for r in range(PEER_TOPK):
            m = jnp.max(cur, axis=0, keepdims=True)
            zsum = zsum + jnp.exp(m - top)
            cur = jnp.where(cur >= m, NEG_INF, cur)
        tau = m
        thr = jnp.full_like(s1, jnp.inf)
        for c in range(PEER_TOPK):
            bc = b[c:c + 1, :]
            thr = jnp.where(s1 + bc >= tau, bc, thr)
        thr_ref[hd] = thr
        e1_ref[hd] = jnp.exp(s1 - a[0:1, :]) * (1.0 / zsum)
        s2_ref[hd] = s2
        e2_ref[hd] = jnp.exp(s2 - b[0:1, :])
        return carry

    lax.fori_loop(0, PEER_HEADS, head_body, 0)


def peer_route(x2, g, w_pq, sub_keys):
    n, d = x2.shape
    tr = PEER_TR
    qd = w_pq.shape[1]
    route_shape = jax.ShapeDtypeStruct((PEER_HEADS, PEER_KEYS, n), F32)
    route_spec = pl.BlockSpec((PEER_HEADS, PEER_KEYS, tr), lambda i: (0, 0, i))
    return pl.pallas_call(
        _peer_route_kernel,
        grid=(n // tr,),
        in_specs=[
            pl.BlockSpec((tr, d), lambda i: (i, 0)),
            pl.BlockSpec((1, d), lambda i: (0, 0)),
            pl.BlockSpec((d, qd), lambda i: (0, 0)),
            pl.BlockSpec((2, PEER_KEYS, PEER_KEYS), lambda i: (0, 0, 0)),
        ],
        out_specs=[pl.BlockSpec((tr, d), lambda i: (i, 0))] + [route_spec] * 4,
        out_shape=[jax.ShapeDtypeStruct((n, d), BF16)] + [route_shape] * 4,
        scratch_shapes=[
            pltpu.VMEM((2 * PEER_HEADS, tr, PEER_KEYS), BF16),
            pltpu.VMEM((PEER_TOPK, tr), F32),
            pltpu.VMEM((PEER_TOPK, tr), F32),
        ],
        compiler_params=_cparams(1),
        name="peer_route",
    )(x2, g, w_pq, sub_keys)


PEER_TN = 512
PEER_EB = 1024


def _gelu_tanh(a):
    inner = math.sqrt(2.0 / math.pi) * (a + 0.044715 * (a * a * a))
    return 0.5 * a * (1.0 + jnp.tanh(inner))


def _peer_dense_kernel(h_ref, x_ref, u_ref, vt_ref, thr_ref, e1_ref, s2_ref, e2_ref, gf_ref,
                       o_ref, at_scr, g_scr, acc_scr):
    e = pl.program_id(1)
    tn = h_ref.shape[0]

    @pl.when(e == 0)
    def _():
        acc_scr[...] = jnp.zeros_like(acc_scr)

    at_scr[...] = lax.dot_general(u_ref[...], h_ref[...], (((1,), (1,)), ((), ())),
                                  preferred_element_type=F32)
    for ii in range(PEER_EB // PEER_KEYS):
        rows = slice(ii * PEER_KEYS, (ii + 1) * PEER_KEYS)
        for lc in range(tn // LANES):
            cols = slice(lc * LANES, (lc + 1) * LANES)
            wgt = jnp.zeros((PEER_KEYS, LANES), F32)
            for hd in range(PEER_HEADS):
                sel = s2_ref[hd, :, cols] >= thr_ref[hd, ii:ii + 1, cols]
                wgt = wgt + jnp.where(sel, e2_ref[hd, :, cols], 0.0) * e1_ref[hd, ii:ii + 1, cols]
            g_scr[rows, cols] = (_gelu_tanh(at_scr[rows, cols]) * wgt).astype(BF16)
    acc_scr[...] += jnp.dot(vt_ref[...], g_scr[...], preferred_element_type=F32)

    @pl.when(e == pl.num_programs(1) - 1)
    def _():
        x3 = x_ref[...] + acc_scr[...].T
        o_ref[...] = _rms(x3, gf_ref[...])


def peer_dense(h3, x2, u, vt, thr, e1, s2, e2, g_final):
    n, d = x2.shape
    ne = u.shape[0]
    tn, eb = PEER_TN, PEER_EB
    ib = eb // PEER_KEYS
    tok_spec = pl.BlockSpec((PEER_HEADS, PEER_KEYS, tn), lambda i, e: (0, 0, i))
    blk_spec = pl.BlockSpec((PEER_HEADS, ib, tn), lambda i, e: (0, e, i))
    return pl.pallas_call(
        _peer_dense_kernel,
        grid=(n // tn, ne // eb),
        in_specs=[
            pl.BlockSpec((tn, d), lambda i, e: (i, 0)),
            pl.BlockSpec((tn, d), lambda i, e: (i, 0)),
            pl.BlockSpec((eb, d), lambda i, e: (e, 0)),
            pl.BlockSpec((d, eb), lambda i, e: (0, e)),
            blk_spec, blk_spec, tok_spec, tok_spec,
            pl.BlockSpec((1, d), lambda i, e: (0, 0)),
        ],
        out_specs=pl.BlockSpec((tn, d), lambda i, e: (i, 0)),
        out_shape=jax.ShapeDtypeStruct((n, d), F32),
        scratch_shapes=[
            pltpu.VMEM((eb, tn), F32),
            pltpu.VMEM((eb, tn), BF16),
            pltpu.VMEM((d, tn), F32),
        ],
        compiler_params=_cparams(2),
        name="peer_dense",
    )(h3, x2, u, vt, thr, e1, s2, e2, g_final)


def kernel(x, mem, norm_mix_g, w_in, b_gate, conv_dw_w, conv_dw_b, conv_ln_g, conv_ln_b,
           w_conv_out, w_attn_out, w_out, norm_xattn_g, norm_mem_g, w_xq, w_xkv, w_xo,
           norm_ffn_g, w_peer_q, peer_sub_keys, peer_u, peer_v, norm_final_g):
    batch, seq, d = x.shape
    mem_len = mem.shape[1]
    depth = w_in.shape[0]
    c = conv_dw_w.shape[2]
    d_att = w_attn_out.shape[1]
    n = batch * seq
    xf = x.reshape(n, d)
    memf = mem.reshape(batch * mem_len, d)
    row = lambda v: v.reshape(1, -1).astype(F32)
    q_col = 2 * c // LANES
    k_col = q_col + d_att // LANES
    v_col = k_col + d_att // LANES
    gate_col = (2 * c + 3 * d_att) // d

    for l in range(depth):
        proj = norm_matmul(xf, row(norm_mix_g[l]), w_in[l].astype(BF16), 1024, 1024)
        y = dwconv(proj, conv_dw_w[l], row(conv_dw_b[l]), batch, seq, 512)
        gc = conv_out(y, proj, row(conv_ln_g[l]), row(conv_ln_b[l]), row(b_gate[l, :d]),
                      w_conv_out[l].astype(BF16), gate_col, 512)
        attn = stick_breaking(proj, batch, seq, q_col, k_col, v_col)
        x1 = merge(attn, gc, proj, xf, row(b_gate[l, d:]), w_attn_out[l].astype(BF16),
                   w_out[l].astype(BF16), gate_col + 1, 512)
        kv = norm_matmul(memf, row(norm_mem_g[l]), w_xkv[l].astype(BF16), 256, 1024)
        x2 = cross_attention(x1, row(norm_xattn_g[l]), kv, w_xq[l].astype(BF16),
                             w_xo[l].astype(BF16), seq, mem_len, 512)
        h3, thr, e1, s2, e2 = peer_route(x2, row(norm_ffn_g[l]), w_peer_q[l].astype(BF16),
                                         peer_sub_keys[l].astype(BF16))
        last = l == depth - 1
        gf = row(norm_final_g) if last else None
        assert last, "the final RMSNorm is fused into the last layer's PEER stage"
        xf = peer_dense(h3, x2, peer_u[l].astype(BF16), peer_v[l].T.astype(BF16),
                        thr, e1, s2, e2, gf)
    return xf.reshape(batch, seq, d)
```

```python
import functools
import math

import jax
import jax.numpy as jnp
from jax import lax
from jax.experimental import pallas as pl
from jax.experimental.pallas import tpu as pltpu

F32 = jnp.float32
BF16 = jnp.bfloat16

D_MODEL = 1024
CONV_WIDTH = 31
SB_HEADS = 16
SB_HEAD_DIM = 64
XA_HEADS = 4
XA_HEAD_DIM = 256
PEER_HEADS = 8
PEER_KEYS = 128
PEER_TOPK = 16
EPS = 1e-6

LANES = 128
VMEM_LIMIT = 56 * 1024 * 1024

NEG_INF = float("-inf")


def _cparams(n_axes):
    return pltpu.CompilerParams(
        dimension_semantics=("arbitrary",) * n_axes, vmem_limit_bytes=VMEM_LIMIT)


def _rms(x, g):
    ms = jnp.mean(x * x, axis=-1, keepdims=True)
    return x * lax.rsqrt(ms + EPS) * g


def _norm_mm_kernel(x_ref, g_ref, w_ref, o_ref, h_scr):
    @pl.when(pl.program_id(1) == 0)
    def _():
        h_scr[...] = _rms(x_ref[...], g_ref[...]).astype(BF16)

    o_ref[...] = jnp.dot(h_scr[...], w_ref[...], preferred_element_type=F32).astype(o_ref.dtype)


def norm_matmul(x, g, w, tm, tn):
    n, d = x.shape
    m = w.shape[1]
    return pl.pallas_call(
        _norm_mm_kernel,
        grid=(n // tm, m // tn),
        in_specs=[
            pl.BlockSpec((tm, d), lambda i, j: (i, 0)),
            pl.BlockSpec((1, d), lambda i, j: (0, 0)),
            pl.BlockSpec((d, tn), lambda i, j: (0, j)),
        ],
        out_specs=pl.BlockSpec((tm, tn), lambda i, j: (i, j)),
        out_shape=jax.ShapeDtypeStruct((n, m), BF16),
        scratch_shapes=[pltpu.VMEM((tm, d), BF16)],
        compiler_params=_cparams(2),
        name="norm_matmul",
    )(x, g, w)


CONV_HALO = 32
CONV_ROWS = 64


def _dwconv_kernel(a_ref, gt_ref, ah_ref, gh_ref, w_ref, b_ref, y_ref, u_scr):
    ts = a_ref.shape[0]
    first = pl.program_id(1) == 0
    u_scr[CONV_HALO:CONV_HALO + ts, :] = (
        a_ref[...].astype(F32) * jax.nn.sigmoid(gt_ref[...].astype(F32)))
    uh = ah_ref[...].astype(F32) * jax.nn.sigmoid(gh_ref[...].astype(F32))
    u_scr[0:CONV_HALO, :] = jnp.where(first, 0.0, uh)
    off = CONV_HALO - (CONV_WIDTH - 1)
    for r in range(ts // CONV_ROWS):
        r0 = r * CONV_ROWS
        acc = jnp.zeros((CONV_ROWS, LANES), F32) + b_ref[...]
        for k in range(CONV_WIDTH):
            acc = acc + w_ref[k:k + 1, :] * u_scr[r0 + off + k:r0 + off + k + CONV_ROWS, :]
        y_ref[r0:r0 + CONV_ROWS, :] = acc


def dwconv(proj, dw_w, dw_b, batch, seq, ts):
    n = proj.shape[0]
    c = dw_w.shape[1]
    nct = c // LANES
    spt = seq // ts
    hb = ts // CONV_HALO

    def halo_idx(col0):
        def f(b, i, cc):
            return (jnp.maximum((b * spt + i) * hb - 1, 0), col0 + cc)
        return f

    return pl.pallas_call(
        _dwconv_kernel,
        grid=(batch, spt, nct),
        in_specs=[
            pl.BlockSpec((ts, LANES), lambda b, i, cc: (b * spt + i, cc)),
            pl.BlockSpec((ts, LANES), lambda b, i, cc: (b * spt + i, nct + cc)),
            pl.BlockSpec((CONV_HALO, LANES), halo_idx(0)),
            pl.BlockSpec((CONV_HALO, LANES), halo_idx(nct)),
            pl.BlockSpec((CONV_WIDTH, LANES), lambda b, i, cc: (0, cc)),
            pl.BlockSpec((1, LANES), lambda b, i, cc: (0, cc)),
        ],
        out_specs=pl.BlockSpec((ts, LANES), lambda b, i, cc: (b * spt + i, cc)),
        out_shape=jax.ShapeDtypeStruct((n, c), F32),
        scratch_shapes=[pltpu.VMEM((CONV_HALO + ts, LANES), F32)],
        compiler_params=_cparams(3),
        name="dwconv",
    )(proj, proj, proj, proj, dw_w, dw_b)


def _conv_out_kernel(y_ref, gl_ref, lng_ref, lnb_ref, bg_ref, w_ref, o_ref):
    y = y_ref[...]
    mu = jnp.mean(y, axis=-1, keepdims=True)
    yc = y - mu
    var = jnp.mean(yc * yc, axis=-1, keepdims=True)
    yn = yc * lax.rsqrt(var + EPS) * lng_ref[...] + lnb_ref[...]
    act = (yn * jax.nn.sigmoid(yn)).astype(BF16)
    co = jnp.dot(act, w_ref[...], preferred_element_type=F32)
    gate = jax.nn.sigmoid(gl_ref[...].astype(F32) + bg_ref[...])
    o_ref[...] = (gate * co).astype(o_ref.dtype)


def conv_out(y, proj, ln_g, ln_b, bg_conv, w, gate_col_block, tm):
    n, c = y.shape
    d = w.shape[1]
    return pl.pallas_call(
        _conv_out_kernel,
        grid=(n // tm,),
        in_specs=[
            pl.BlockSpec((tm, c), lambda i: (i, 0)),
            pl.BlockSpec((tm, d), lambda i: (i, gate_col_block)),
            pl.BlockSpec((1, c), lambda i: (0, 0)),
            pl.BlockSpec((1, c), lambda i: (0, 0)),
            pl.BlockSpec((1, d), lambda i: (0, 0)),
            pl.BlockSpec((c, d), lambda i: (0, 0)),
        ],
        out_specs=pl.BlockSpec((tm, d), lambda i: (i, 0)),
        out_shape=jax.ShapeDtypeStruct((n, d), BF16),
        compiler_params=_cparams(1),
        name="conv_out",
    )(y, proj, ln_g, ln_b, bg_conv, w)


SB_TQ = 256
SB_TK = 256
SB_CHUNK = 32


def _softplus(z):
    return jnp.maximum(z, 0.0) + jnp.log(1.0 + jnp.exp(-jnp.abs(z)))


def _sb_kernel(q_ref, k_ref, v_ref, tri_ref, o_ref,
               z_scr, hl_scr, w_scr, acc_scr, run_scr, rs_scr):
    qi = pl.program_id(2)
    tq, tk = SB_TQ, SB_TK
    lane = lax.broadcasted_iota(jnp.int32, (1, LANES), 1)
    q2 = q_ref[...]
    scale = jnp.asarray(1.0 / math.sqrt(SB_HEAD_DIM), q2.dtype)
    qh = [jnp.where((lane >= hd * SB_HEAD_DIM) & (lane < (hd + 1) * SB_HEAD_DIM),
                    q2, jnp.zeros_like(q2)) * scale for hd in range(2)]
    chunks = [slice(c * SB_CHUNK, (c + 1) * SB_CHUNK) for c in range(tq // SB_CHUNK)]

    def causal(c):
        qpos = c * SB_CHUNK + lax.broadcasted_iota(jnp.int32, (SB_CHUNK, tk), 0)
        kpos = lax.broadcasted_iota(jnp.int32, (SB_CHUNK, tk), 1)
        return kpos < qpos

    def region(js, diags):
        slots = range(len(js))
        kbs = [k_ref[pl.ds(pl.multiple_of(j * tk, tk), tk), :] for j in js]
        vbs = [v_ref[pl.ds(pl.multiple_of(j * tk, tk), tk), :] for j in js]
        for s in slots:
            for hd in range(2):
                z_scr[hd, s] = lax.dot_general(qh[hd], kbs[s], (((1,), (1,)), ((), ())),
                                               preferred_element_type=F32)
        for s in slots:
            for hd in range(2):
                for c, rows in enumerate(chunks):
                    z = z_scr[hd, s, rows, :]
                    sp = _softplus(z)
                    z_scr[hd, s, rows, :] = z - sp
                    if diags[s]:
                        sp = jnp.where(causal(c), sp, 0.0)
                    hi = sp.astype(BF16)
                    hl_scr[hd, s, rows, :tk] = hi
                    hl_scr[hd, s, rows, tk:] = (sp - hi.astype(F32)).astype(BF16)
                    rs_scr[hd, s, rows, :] = jnp.sum(sp, axis=-1, keepdims=True)
        for s in slots:
            for hd in range(2):
                after = jnp.dot(hl_scr[hd, s], tri_ref[...], preferred_element_type=F32)
                for c, rows in enumerate(chunks):
                    logw = z_scr[hd, s, rows, :] + after[rows, :]
                    if diags[s]:
                        w = jnp.where(causal(c), jnp.exp(logw), 0.0)
                        run_scr[hd, rows, :] = -rs_scr[hd, s, rows, :]
                    else:
                        run = run_scr[hd, rows, :]
                        w = jnp.exp(logw + run)
                        run_scr[hd, rows, :] = run - rs_scr[hd, s, rows, :]
                    w_scr[hd, s, rows, :] = w.astype(BF16)
        for hd in range(2):
            for s in slots:
                pv = jnp.dot(w_scr[hd, s], vbs[s], preferred_element_type=F32)
                if diags[s]:
                    acc_scr[hd] = pv
                else:
                    acc_scr[hd] += pv

    odd = qi % 2

    @pl.when(odd == 0)
    def _():
        region([qi], [True])

    @pl.when(odd == 1)
    def _():
        region([qi, qi - 1], [True, False])

    def body(p, carry):
        j = qi - odd - 1 - 2 * p
        region([j, j - 1], [False, False])
        return carry

    lax.fori_loop(0, (qi - odd) // 2, body, 0)
    o_ref[...] = jnp.where(lane < SB_HEAD_DIM, acc_scr[0], acc_scr[1]).astype(o_ref.dtype)


def _neg_tri(tk):
    j = lax.broadcasted_iota(jnp.int32, (2 * tk, tk), 0) % tk
    s = lax.broadcasted_iota(jnp.int32, (2 * tk, tk), 1)
    return jnp.where(j > s, -1.0, 0.0).astype(BF16)


def stick_breaking(proj, batch, seq, q_col, k_col, v_col):
    n = proj.shape[0]
    nq = seq // SB_TQ
    npair = SB_HEADS * SB_HEAD_DIM // LANES
    tri = _neg_tri(SB_TK)
    return pl.pallas_call(
        _sb_kernel,
        grid=(batch, npair, nq),
        in_specs=[
            pl.BlockSpec((SB_TQ, LANES), lambda b, hp, qi: (b * nq + qi, q_col + hp)),
            pl.BlockSpec((seq, LANES), lambda b, hp, qi: (b, k_col + hp)),
            pl.BlockSpec((seq, LANES), lambda b, hp, qi: (b, v_col + hp)),
            pl.BlockSpec((2 * SB_TK, SB_TK), lambda b, hp, qi: (0, 0)),
        ],
        out_specs=pl.BlockSpec((SB_TQ, LANES), lambda b, hp, qi: (b * nq + qi, hp)),
        out_shape=jax.ShapeDtypeStruct((n, SB_HEADS * SB_HEAD_DIM), BF16),
        scratch_shapes=[
            pltpu.VMEM((2, 2, SB_TQ, SB_TK), F32),
            pltpu.VMEM((2, 2, SB_TQ, 2 * SB_TK), BF16),
            pltpu.VMEM((2, 2, SB_TQ, SB_TK), BF16),
            pltpu.VMEM((2, SB_TQ, LANES), F32),
            pltpu.VMEM((2, SB_TQ, 1), F32),
            pltpu.VMEM((2, 2, SB_TQ, 1), F32),
        ],
        compiler_params=_cparams(3),
        name="stick_breaking",
    )(proj, proj, proj, tri)


def _merge_kernel(attn_ref, gc_ref, gl_ref, x_ref, bg_ref, wa_ref, wo_ref, o_ref):
    ao = jnp.dot(attn_ref[...], wa_ref[...], preferred_element_type=F32)
    gate = jax.nn.sigmoid(gl_ref[...].astype(F32) + bg_ref[...])
    mixed = (gc_ref[...].astype(F32) + gate * ao).astype(BF16)
    o_ref[...] = x_ref[...] + jnp.dot(mixed, wo_ref[...], preferred_element_type=F32)


def merge(attn, gc, proj, x, bg_attn, w_attn_out, w_out, gate_col_block, tm):
    n, d = x.shape
    da = attn.shape[1]
    return pl.pallas_call(
        _merge_kernel,
        grid=(n // tm,),
        in_specs=[
            pl.BlockSpec((tm, da), lambda i: (i, 0)),
            pl.BlockSpec((tm, d), lambda i: (i, 0)),
            pl.BlockSpec((tm, d), lambda i: (i, gate_col_block)),
            pl.BlockSpec((tm, d), lambda i: (i, 0)),
            pl.BlockSpec((1, d), lambda i: (0, 0)),
            pl.BlockSpec((da, d), lambda i: (0, 0)),
            pl.BlockSpec((d, d), lambda i: (0, 0)),
        ],
        out_specs=pl.BlockSpec((tm, d), lambda i: (i, 0)),
        out_shape=jax.ShapeDtypeStruct((n, d), F32),
        compiler_params=_cparams(1),
        name="merge",
    )(attn, gc, proj, x, bg_attn, w_attn_out, w_out)


def _xattn_kernel(x_ref, g_ref, kv_ref, wq_ref, wo_ref, o_ref):
    x = x_ref[...]
    h = _rms(x, g_ref[...]).astype(BF16)
    q = jnp.dot(h, wq_ref[...], preferred_element_type=F32)
    q = (q * (1.0 / math.sqrt(XA_HEAD_DIM))).astype(BF16)
    d = XA_HEADS * XA_HEAD_DIM
    outs = []
    for hd in range(XA_HEADS):
        cols = slice(hd * XA_HEAD_DIM, (hd + 1) * XA_HEAD_DIM)
        kh = kv_ref[:, cols]
        vh = kv_ref[:, d + hd * XA_HEAD_DIM:d + (hd + 1) * XA_HEAD_DIM]
        sc = lax.dot_general(q[:, cols], kh, (((1,), (1,)), ((), ())),
                             preferred_element_type=F32)
        p = jnp.exp(sc - jnp.max(sc, axis=-1, keepdims=True))
        p = (p / jnp.sum(p, axis=-1, keepdims=True)).astype(BF16)
        outs.append(jnp.dot(p, vh, preferred_element_type=F32).astype(BF16))
    o = jnp.concatenate(outs, axis=1)
    o_ref[...] = x + jnp.dot(o, wo_ref[...], preferred_element_type=F32)


def cross_attention(x1, g, kv, w_xq, w_xo, seq, mem_len, tm):
    n, d = x1.shape
    tiles_per_batch = seq // tm
    return pl.pallas_call(
        _xattn_kernel,
        grid=(n // tm,),
        in_specs=[
            pl.BlockSpec((tm, d), lambda i: (i, 0)),
            pl.BlockSpec((1, d), lambda i: (0, 0)),
            pl.BlockSpec((mem_len, 2 * d), lambda i: (i // tiles_per_batch, 0)),
            pl.BlockSpec((d, d), lambda i: (0, 0)),
            pl.BlockSpec((d, d), lambda i: (0, 0)),
        ],
        out_specs=pl.BlockSpec((tm, d), lambda i: (i, 0)),
        out_shape=jax.ShapeDtypeStruct((n, d), F32),
        compiler_params=_cparams(1),
        name="cross_attention",
    )(x1, g, kv, w_xq, w_xo)


PEER_TR = 256
RANK_REST = 127.0


def _peer_route_kernel(x_ref, g_ref, wq_ref, sk_ref, h_ref, cnt_ref, e1_ref, rank_ref, e2_ref,
                       q_scr, a_scr, b_scr):
    h3 = _rms(x_ref[...], g_ref[...]).astype(BF16)
    h_ref[...] = h3
    q = jnp.dot(h3, wq_ref[...], preferred_element_type=F32).astype(BF16)
    for c in range(2 * PEER_HEADS):
        q_scr[c] = q[:, c * PEER_KEYS:(c + 1) * PEER_KEYS]

    def head_body(hd, carry):
        dn = (((1,), (1,)), ((), ()))
        s1_all = lax.dot_general(sk_ref[0], q_scr[2 * hd], dn, preferred_element_type=F32)
        s2_all = lax.dot_general(sk_ref[1], q_scr[2 * hd + 1], dn, preferred_element_type=F32)
        for lc in range(PEER_TR // LANES):
            cols = slice(lc * LANES, (lc + 1) * LANES)
            s1 = s1_all[:, cols]
            s2 = s2_all[:, cols]
            cur = s1
            for r in range(PEER_TOPK):
                m = jnp.max(cur, axis=0, keepdims=True)
                a_scr[r:r + 1, :] = m
                cur = jnp.where(cur >= m, NEG_INF, cur)
            cur = s2
            rank = jnp.full_like(s2, RANK_REST)
            for r in range(PEER_TOPK):
                m = jnp.max(cur, axis=0, keepdims=True)
                b_scr[r:r + 1, :] = m
                hit = cur >= m
                rank = jnp.where(hit, float(r), rank)
                cur = jnp.where(hit, NEG_INF, cur)
            a = a_scr[...]
            b = b_scr[...]
            blocks = [a[0:1, :] + b]
            for r in range(1, 8):
                blocks.append(a[r:r + 1, :] + b[0:8, :])
            blocks.append(a[8:16, :] + b[0:1, :])
            cur = jnp.concatenate(blocks, axis=0)
            top = a[0:1, :] + b[0:1, :]
            zsum = jnp.zeros_like(top)
            m = top
            for r in range(PEER_TOPK):
                m = jnp.max(cur, axis=0, keepdims=True)
                zsum = zsum + jnp.exp(m - top)
                cur = jnp.where(cur >= m, NEG_INF, cur)
            tau = m
            cnt = jnp.zeros_like(s1)
            for c in range(PEER_TOPK):
                cnt = jnp.where(s1 + b[c:c + 1, :] >= tau, float(c + 1), cnt)
            cnt_ref[hd, :, cols] = cnt
            e1_ref[hd, :, cols] = jnp.exp(s1 - a[0:1, :]) * (1.0 / zsum)
            rank_ref[hd, :, cols] = pltpu.bitcast(rank.astype(BF16), jnp.uint32)
            e2_ref[hd, :, cols] = pltpu.bitcast(jnp.exp(s2 - b[0:1, :]).astype(BF16), jnp.uint32)
        return carry

    lax.fori_loop(0, PEER_HEADS, head_body, 0)


def peer_route(x2, g, w_pq, sub_keys):
    n, d = x2.shape
    tr = PEER_TR
    qd = w_pq.shape[1]
    f32_shape = jax.ShapeDtypeStruct((PEER_HEADS, PEER_KEYS, n), F32)
    u32_shape = jax.ShapeDtypeStruct((PEER_HEADS, PEER_KEYS // 2, n), jnp.uint32)
    f32_spec = pl.BlockSpec((PEER_HEADS, PEER_KEYS, tr), lambda i: (0, 0, i))
    u32_spec = pl.BlockSpec((PEER_HEADS, PEER_KEYS // 2, tr), lambda i: (0, 0, i))
    return pl.pallas_call(
        _peer_route_kernel,
        grid=(n // tr,),
        in_specs=[
            pl.BlockSpec((tr, d), lambda i: (i, 0)),
            pl.BlockSpec((1, d), lambda i: (0, 0)),
            pl.BlockSpec((d, qd), lambda i: (0, 0)),
            pl.BlockSpec((2, PEER_KEYS, PEER_KEYS), lambda i: (0, 0, 0)),
        ],
        out_specs=[pl.BlockSpec((tr, d), lambda i: (i, 0)),
                   f32_spec, f32_spec, u32_spec, u32_spec],
        out_shape=[jax.ShapeDtypeStruct((n, d), BF16),
                   f32_shape, f32_shape, u32_shape, u32_shape],
        scratch_shapes=[
            pltpu.VMEM((2 * PEER_HEADS, tr, PEER_KEYS), BF16),
            pltpu.VMEM((PEER_TOPK, LANES), F32),
            pltpu.VMEM((PEER_TOPK, LANES), F32),
        ],
        compiler_params=_cparams(1),
        name="peer_route",
    )(x2, g, w_pq, sub_keys)


PEER_TN = 512
PEER_EB = 1024


def _gelu_tanh(a):
    inner = math.sqrt(2.0 / math.pi) * (a + 0.044715 * (a * a * a))
    return 0.5 * a * (1.0 + jnp.tanh(inner))


BF16_ROWS = 16


def _peer_dense_kernel(h_ref, x_ref, u_ref, vt_ref, cnt_ref, e1_ref, rank_ref, e2_ref, gf_ref,
                       o_ref, at_scr, g_scr, acc_scr):
    e = pl.program_id(1)
    last = pl.num_programs(1) - 1
    tn = h_ref.shape[0]
    reps = PEER_KEYS // BF16_ROWS

    at_scr[...] = lax.dot_general(u_ref[...], h_ref[...], (((1,), (1,)), ((), ())),
                                  preferred_element_type=F32)

    def row_bf16(ref, hd, ii, cols):
        r16 = jnp.broadcast_to(ref[hd, ii:ii + 1, cols], (BF16_ROWS, LANES)).astype(BF16)
        return jnp.concatenate([r16] * reps, axis=0)

    for ii in range(PEER_EB // PEER_KEYS):
        rows = slice(ii * PEER_KEYS, (ii + 1) * PEER_KEYS)
        for lc in range(tn // LANES):
            cols = slice(lc * LANES, (lc + 1) * LANES)
            wgt = jnp.zeros((PEER_KEYS, LANES), BF16)
            for hd in range(PEER_HEADS):
                rank = pltpu.bitcast(rank_ref[hd, :, cols], BF16)
                e2 = pltpu.bitcast(e2_ref[hd, :, cols], BF16)
                kept = jnp.where(rank < row_bf16(cnt_ref, hd, ii, cols), e2,
                                 jnp.zeros((), BF16))
                wgt = wgt + kept * row_bf16(e1_ref, hd, ii, cols)
            g_scr[rows, cols] = _gelu_tanh(at_scr[rows, cols]).astype(BF16) * wgt
    out_t = jnp.dot(vt_ref[...], g_scr[...], preferred_element_type=F32)

    @pl.when(e == 0)
    def _():
        acc_scr[...] = out_t

    @pl.when(e > 0)
    def _():
        acc_scr[...] += out_t

    @pl.when(e == last)
    def _():
        x3 = x_ref[...] + acc_scr[...].T
        o_ref[...] = _rms(x3, gf_ref[...])


def peer_dense(h3, x2, u, vt, cnt, e1, rank, e2, g_final):
    n, d = x2.shape
    ne = u.shape[0]
    tn, eb = PEER_TN, PEER_EB
    ib = eb // PEER_KEYS
    tok_spec = pl.BlockSpec((PEER_HEADS, PEER_KEYS // 2, tn), lambda i, e: (0, 0, i))
    blk_spec = pl.BlockSpec((PEER_HEADS, ib, tn), lambda i, e: (0, e, i))
    return pl.pallas_call(
        _peer_dense_kernel,
        grid=(n // tn, ne // eb),
        in_specs=[
            pl.BlockSpec((tn, d), lambda i, e: (i, 0)),
            pl.BlockSpec((tn, d), lambda i, e: (i, 0)),
            pl.BlockSpec((eb, d), lambda i, e: (e, 0)),
            pl.BlockSpec((d, eb), lambda i, e: (0, e)),
            blk_spec, blk_spec, tok_spec, tok_spec,
            pl.BlockSpec((1, d), lambda i, e: (0, 0)),
        ],
        out_specs=pl.BlockSpec((tn, d), lambda i, e: (i, 0)),
        out_shape=jax.ShapeDtypeStruct((n, d), F32),
        scratch_shapes=[
            pltpu.VMEM((eb, tn), F32),
            pltpu.VMEM((eb, tn), BF16),
            pltpu.VMEM((d, tn), F32),
        ],
        compiler_params=_cparams(2),
        name="peer_dense",
    )(h3, x2, u, vt, cnt, e1, rank, e2, g_final)


def kernel(x, mem, norm_mix_g, w_in, b_gate, conv_dw_w, conv_dw_b, conv_ln_g, conv_ln_b,
           w_conv_out, w_attn_out, w_out, norm_xattn_g, norm_mem_g, w_xq, w_xkv, w_xo,
           norm_ffn_g, w_peer_q, peer_sub_keys, peer_u, peer_v, norm_final_g):
    batch, seq, d = x.shape
    mem_len = mem.shape[1]
    depth = w_in.shape[0]
    c = conv_dw_w.shape[2]
    d_att = w_attn_out.shape[1]
    n = batch * seq
    xf = x.reshape(n, d)
    memf = mem.reshape(batch * mem_len, d)
    row = lambda v: v.reshape(1, -1).astype(F32)
    q_col = 2 * c // LANES
    k_col = q_col + d_att // LANES
    v_col = k_col + d_att // LANES
    gate_col = (2 * c + 3 * d_att) // d

    for l in range(depth):
        proj = norm_matmul(xf, row(norm_mix_g[l]), w_in[l].astype(BF16), 1024, 1024)
        y = dwconv(proj, conv_dw_w[l], row(conv_dw_b[l]), batch, seq, 512)
        gc = conv_out(y, proj, row(conv_ln_g[l]), row(conv_ln_b[l]), row(b_gate[l, :d]),
                      w_conv_out[l].astype(BF16), gate_col, 512)
        attn = stick_breaking(proj, batch, seq, q_col, k_col, v_col)
        x1 = merge(attn, gc, proj, xf, row(b_gate[l, d:]), w_attn_out[l].astype(BF16),
                   w_out[l].astype(BF16), gate_col + 1, 512)
        kv = norm_matmul(memf, row(norm_mem_g[l]), w_xkv[l].astype(BF16), 256, 1024)
        x2 = cross_attention(x1, row(norm_xattn_g[l]), kv, w_xq[l].astype(BF16),
                             w_xo[l].astype(BF16), seq, mem_len, 512)
        h3, cnt, e1, rank, e2 = peer_route(x2, row(norm_ffn_g[l]), w_peer_q[l].astype(BF16),
                                           peer_sub_keys[l].astype(BF16))
        last = l == depth - 1
        gf = row(norm_final_g) if last else None
        assert last, "the final RMSNorm is fused into the last layer's PEER stage"
        xf = peer_dense(h3, x2, peer_u[l].astype(BF16), peer_v[l].T.astype(BF16),
                        cnt, e1, rank, e2, gf)
    return xf.reshape(batch, seq, d)
```

```python
import functools
import math

import jax
import jax.numpy as jnp
from jax import lax
from jax.experimental import pallas as pl
from jax.experimental.pallas import tpu as pltpu

F32 = jnp.float32
BF16 = jnp.bfloat16

D_MODEL = 1024
CONV_WIDTH = 31
SB_HEADS = 16
SB_HEAD_DIM = 64
XA_HEADS = 4
XA_HEAD_DIM = 256
PEER_HEADS = 8
PEER_KEYS = 128
PEER_TOPK = 16
EPS = 1e-6

LANES = 128
VMEM_LIMIT = 56 * 1024 * 1024

NEG_INF = float("-inf")


def _cparams(n_axes):
    return pltpu.CompilerParams(
        dimension_semantics=("arbitrary",) * n_axes, vmem_limit_bytes=VMEM_LIMIT)


def _rms(x, g):
    ms = jnp.mean(x * x, axis=-1, keepdims=True)
    return x * lax.rsqrt(ms + EPS) * g


def _norm_mm_kernel(x_ref, g_ref, w_ref, o_ref, h_scr):
    @pl.when(pl.program_id(1) == 0)
    def _():
        h_scr[...] = _rms(x_ref[...], g_ref[...]).astype(BF16)

    o_ref[...] = jnp.dot(h_scr[...], w_ref[...], preferred_element_type=F32).astype(o_ref.dtype)


def norm_matmul(x, g, w, tm, tn):
    n, d = x.shape
    m = w.shape[1]
    return pl.pallas_call(
        _norm_mm_kernel,
        grid=(n // tm, m // tn),
        in_specs=[
            pl.BlockSpec((tm, d), lambda i, j: (i, 0)),
            pl.BlockSpec((1, d), lambda i, j: (0, 0)),
            pl.BlockSpec((d, tn), lambda i, j: (0, j)),
        ],
        out_specs=pl.BlockSpec((tm, tn), lambda i, j: (i, j)),
        out_shape=jax.ShapeDtypeStruct((n, m), BF16),
        scratch_shapes=[pltpu.VMEM((tm, d), BF16)],
        compiler_params=_cparams(2),
        name="norm_matmul",
    )(x, g, w)


CONV_HALO = 32
CONV_ROWS = 64


def _dwconv_kernel(a_ref, gt_ref, ah_ref, gh_ref, w_ref, b_ref, y_ref, u_scr):
    ts = a_ref.shape[0]
    first = pl.program_id(1) == 0
    u_scr[CONV_HALO:CONV_HALO + ts, :] = (
        a_ref[...].astype(F32) * jax.nn.sigmoid(gt_ref[...].astype(F32)))
    uh = ah_ref[...].astype(F32) * jax.nn.sigmoid(gh_ref[...].astype(F32))
    u_scr[0:CONV_HALO, :] = jnp.where(first, 0.0, uh)
    off = CONV_HALO - (CONV_WIDTH - 1)
    for r in range(ts // CONV_ROWS):
        r0 = r * CONV_ROWS
        acc = jnp.zeros((CONV_ROWS, LANES), F32) + b_ref[...]
        for k in range(CONV_WIDTH):
            acc = acc + w_ref[k:k + 1, :] * u_scr[r0 + off + k:r0 + off + k + CONV_ROWS, :]
        y_ref[r0:r0 + CONV_ROWS, :] = acc


def dwconv(proj, dw_w, dw_b, batch, seq, ts):
    n = proj.shape[0]
    c = dw_w.shape[1]
    nct = c // LANES
    spt = seq // ts
    hb = ts // CONV_HALO

    def halo_idx(col0):
        def f(b, i, cc):
            return (jnp.maximum((b * spt + i) * hb - 1, 0), col0 + cc)
        return f

    return pl.pallas_call(
        _dwconv_kernel,
        grid=(batch, spt, nct),
        in_specs=[
            pl.BlockSpec((ts, LANES), lambda b, i, cc: (b * spt + i, cc)),
            pl.BlockSpec((ts, LANES), lambda b, i, cc: (b * spt + i, nct + cc)),
            pl.BlockSpec((CONV_HALO, LANES), halo_idx(0)),
            pl.BlockSpec((CONV_HALO, LANES), halo_idx(nct)),
            pl.BlockSpec((CONV_WIDTH, LANES), lambda b, i, cc: (0, cc)),
            pl.BlockSpec((1, LANES), lambda b, i, cc: (0, cc)),
        ],
        out_specs=pl.BlockSpec((ts, LANES), lambda b, i, cc: (b * spt + i, cc)),
        out_shape=jax.ShapeDtypeStruct((n, c), F32),
        scratch_shapes=[pltpu.VMEM((CONV_HALO + ts, LANES), F32)],
        compiler_params=_cparams(3),
        name="dwconv",
    )(proj, proj, proj, proj, dw_w, dw_b)


def _conv_out_kernel(y_ref, gl_ref, lng_ref, lnb_ref, bg_ref, w_ref, o_ref):
    y = y_ref[...]
    mu = jnp.mean(y, axis=-1, keepdims=True)
    yc = y - mu
    var = jnp.mean(yc * yc, axis=-1, keepdims=True)
    yn = yc * lax.rsqrt(var + EPS) * lng_ref[...] + lnb_ref[...]
    act = (yn * jax.nn.sigmoid(yn)).astype(BF16)
    co = jnp.dot(act, w_ref[...], preferred_element_type=F32)
    gate = jax.nn.sigmoid(gl_ref[...].astype(F32) + bg_ref[...])
    o_ref[...] = (gate * co).astype(o_ref.dtype)


def conv_out(y, proj, ln_g, ln_b, bg_conv, w, gate_col_block, tm):
    n, c = y.shape
    d = w.shape[1]
    return pl.pallas_call(
        _conv_out_kernel,
        grid=(n // tm,),
        in_specs=[
            pl.BlockSpec((tm, c), lambda i: (i, 0)),
            pl.BlockSpec((tm, d), lambda i: (i, gate_col_block)),
            pl.BlockSpec((1, c), lambda i: (0, 0)),
            pl.BlockSpec((1, c), lambda i: (0, 0)),
            pl.BlockSpec((1, d), lambda i: (0, 0)),
            pl.BlockSpec((c, d), lambda i: (0, 0)),
        ],
        out_specs=pl.BlockSpec((tm, d), lambda i: (i, 0)),
        out_shape=jax.ShapeDtypeStruct((n, d), BF16),
        compiler_params=_cparams(1),
        name="conv_out",
    )(y, proj, ln_g, ln_b, bg_conv, w)


SB_TQ = 256
SB_TK = 256
SB_CHUNK = 32
SB_GROUP = 4
SB_GW = SB_GROUP * SB_HEAD_DIM


def _softplus(z):
    return jnp.maximum(z, 0.0) + jnp.log(1.0 + jnp.exp(-jnp.abs(z)))


def _sb_kernel(q_ref, k_ref, v_ref, tri_ref, o_ref, z_scr, hl_scr, w_scr, acc_scr, run_scr):
    qi = pl.program_id(2)
    tq, tk = SB_TQ, SB_TK
    heads = range(SB_GROUP)
    lane = lax.broadcasted_iota(jnp.int32, (1, SB_GW), 1)
    in_head = [(lane >= hd * SB_HEAD_DIM) & (lane < (hd + 1) * SB_HEAD_DIM) for hd in heads]
    qg = q_ref[...]
    scale = jnp.asarray(1.0 / math.sqrt(SB_HEAD_DIM), qg.dtype)
    qh = [jnp.where(in_head[hd], qg, jnp.zeros_like(qg)) * scale for hd in heads]
    chunks = [slice(c * SB_CHUNK, (c + 1) * SB_CHUNK) for c in range(tq // SB_CHUNK)]

    def causal(c):
        qpos = c * SB_CHUNK + lax.broadcasted_iota(jnp.int32, (SB_CHUNK, tk), 0)
        kpos = lax.broadcasted_iota(jnp.int32, (SB_CHUNK, tk), 1)
        return kpos < qpos

    def region(js, diags):
        slots = range(len(js))
        kbs = [k_ref[pl.ds(pl.multiple_of(j * tk, tk), tk), :] for j in js]
        vbs = [v_ref[pl.ds(pl.multiple_of(j * tk, tk), tk), :] for j in js]
        for s in slots:
            for hd in heads:
                z_scr[hd, s] = lax.dot_general(qh[hd], kbs[s], (((1,), (1,)), ((), ())),
                                               preferred_element_type=F32)
        for s in slots:
            for hd in heads:
                for c, rows in enumerate(chunks):
                    sp = _softplus(z_scr[hd, s, rows, :])
                    if diags[s]:
                        sp = jnp.where(causal(c), sp, 0.0)
                    hi = sp.astype(BF16)
                    hl_scr[hd, s, rows, :tk] = hi
                    hl_scr[hd, s, rows, tk:] = (sp - hi.astype(F32)).astype(BF16)
        for s in slots:
            for hd in heads:
                inc = jnp.dot(hl_scr[hd, s], tri_ref[...], preferred_element_type=F32)
                for c, rows in enumerate(chunks):
                    logw = z_scr[hd, s, rows, :] + inc[rows, :]
                    if diags[s]:
                        w = jnp.where(causal(c), jnp.exp(logw), 0.0)
                        run_scr[hd, rows, :] = inc[rows, 0:1]
                    else:
                        run = run_scr[hd, rows, :]
                        w = jnp.exp(logw + run)
                        run_scr[hd, rows, :] = run + inc[rows, 0:1]
                    w_scr[hd, s, rows, :] = w.astype(BF16)
        for hd in heads:
            for s in slots:
                pv = jnp.dot(w_scr[hd, s], vbs[s], preferred_element_type=F32)
                if diags[s]:
                    acc_scr[hd] = pv
                else:
                    acc_scr[hd] += pv

    odd = qi % 2

    @pl.when(odd == 0)
    def _():
        region([qi], [True])

    @pl.when(odd == 1)
    def _():
        region([qi, qi - 1], [True, False])

    def body(p, carry):
        j = qi - odd - 1 - 2 * p
        region([j, j - 1], [False, False])
        return carry

    lax.fori_loop(0, (qi - odd) // 2, body, 0)
    out = acc_scr[0]
    for hd in range(1, SB_GROUP):
        out = jnp.where(in_head[hd], acc_scr[hd], out)
    o_ref[...] = out.astype(o_ref.dtype)


def _neg_tri(tk):
    j = lax.broadcasted_iota(jnp.int32, (2 * tk, tk), 0) % tk
    s = lax.broadcasted_iota(jnp.int32, (2 * tk, tk), 1)
    return jnp.where(j >= s, -1.0, 0.0).astype(BF16)


def stick_breaking(proj, batch, seq, q_col, k_col, v_col):
    n = proj.shape[0]
    nq = seq // SB_TQ
    ngroup = SB_HEADS // SB_GROUP
    tri = _neg_tri(SB_TK)
    return pl.pallas_call(
        _sb_kernel,
        grid=(batch, ngroup, nq),
        in_specs=[
            pl.BlockSpec((SB_TQ, SB_GW), lambda b, g, qi: (b * nq + qi, q_col + g)),
            pl.BlockSpec((seq, SB_GW), lambda b, g, qi: (b, k_col + g)),
            pl.BlockSpec((seq, SB_GW), lambda b, g, qi: (b, v_col + g)),
            pl.BlockSpec((2 * SB_TK, SB_TK), lambda b, g, qi: (0, 0)),
        ],
        out_specs=pl.BlockSpec((SB_TQ, SB_GW), lambda b, g, qi: (b * nq + qi, g)),
        out_shape=jax.ShapeDtypeStruct((n, SB_HEADS * SB_HEAD_DIM), BF16),
        scratch_shapes=[
            pltpu.VMEM((SB_GROUP, 2, SB_TQ, SB_TK), F32),
            pltpu.VMEM((SB_GROUP, 2, SB_TQ, 2 * SB_TK), BF16),
            pltpu.VMEM((SB_GROUP, 2, SB_TQ, SB_TK), BF16),
            pltpu.VMEM((SB_GROUP, SB_TQ, SB_GW), F32),
            pltpu.VMEM((SB_GROUP, SB_TQ, 1), F32),
        ],
        compiler_params=_cparams(3),
        name="stick_breaking",
    )(proj, proj, proj, tri)


def _merge_kernel(attn_ref, gc_ref, gl_ref, x_ref, bg_ref, wa_ref, wo_ref, o_ref):
    ao = jnp.dot(attn_ref[...], wa_ref[...], preferred_element_type=F32)
    gate = jax.nn.sigmoid(gl_ref[...].astype(F32) + bg_ref[...])
    mixed = (gc_ref[...].astype(F32) + gate * ao).astype(BF16)
    o_ref[...] = x_ref[...] + jnp.dot(mixed, wo_ref[...], preferred_element_type=F32)


def merge(attn, gc, proj, x, bg_attn, w_attn_out, w_out, gate_col_block, tm):
    n, d = x.shape
    da = attn.shape[1]
    return pl.pallas_call(
        _merge_kernel,
        grid=(n // tm,),
        in_specs=[
            pl.BlockSpec((tm, da), lambda i: (i, 0)),
            pl.BlockSpec((tm, d), lambda i: (i, 0)),
            pl.BlockSpec((tm, d), lambda i: (i, gate_col_block)),
            pl.BlockSpec((tm, d), lambda i: (i, 0)),
            pl.BlockSpec((1, d), lambda i: (0, 0)),
            pl.BlockSpec((da, d), lambda i: (0, 0)),
            pl.BlockSpec((d, d), lambda i: (0, 0)),
        ],
        out_specs=pl.BlockSpec((tm, d), lambda i: (i, 0)),
        out_shape=jax.ShapeDtypeStruct((n, d), F32),
        compiler_params=_cparams(1),
        name="merge",
    )(attn, gc, proj, x, bg_attn, w_attn_out, w_out)


def _xattn_kernel(x_ref, g_ref, kv_ref, wq_ref, wo_ref, o_ref):
    x = x_ref[...]
    h = _rms(x, g_ref[...]).astype(BF16)
    q = jnp.dot(h, wq_ref[...], preferred_element_type=F32)
    q = (q * (1.0 / math.sqrt(XA_HEAD_DIM))).astype(BF16)
    d = XA_HEADS * XA_HEAD_DIM
    outs = []
    for hd in range(XA_HEADS):
        cols = slice(hd * XA_HEAD_DIM, (hd + 1) * XA_HEAD_DIM)
        kh = kv_ref[:, cols]
        vh = kv_ref[:, d + hd * XA_HEAD_DIM:d + (hd + 1) * XA_HEAD_DIM]
        sc = lax.dot_general(q[:, cols], kh, (((1,), (1,)), ((), ())),
                             preferred_element_type=F32)
        p = jnp.exp(sc - jnp.max(sc, axis=-1, keepdims=True))
        p = (p / jnp.sum(p, axis=-1, keepdims=True)).astype(BF16)
        outs.append(jnp.dot(p, vh, preferred_element_type=F32).astype(BF16))
    o = jnp.concatenate(outs, axis=1)
    o_ref[...] = x + jnp.dot(o, wo_ref[...], preferred_element_type=F32)


def cross_attention(x1, g, kv, w_xq, w_xo, seq, mem_len, tm):
    n, d = x1.shape
    tiles_per_batch = seq // tm
    return pl.pallas_call(
        _xattn_kernel,
        grid=(n // tm,),
        in_specs=[
            pl.BlockSpec((tm, d), lambda i: (i, 0)),
            pl.BlockSpec((1, d), lambda i: (0, 0)),
            pl.BlockSpec((mem_len, 2 * d), lambda i: (i // tiles_per_batch, 0)),
            pl.BlockSpec((d, d), lambda i: (0, 0)),
            pl.BlockSpec((d, d), lambda i: (0, 0)),
        ],
        out_specs=pl.BlockSpec((tm, d), lambda i: (i, 0)),
        out_shape=jax.ShapeDtypeStruct((n, d), F32),
        compiler_params=_cparams(1),
        name="cross_attention",
    )(x1, g, kv, w_xq, w_xo)


PEER_TR = 256
RANK_REST = 127.0


def _peer_route_kernel(x_ref, g_ref, wq_ref, sk_ref, h_ref, cnt_ref, e1_ref, rank_ref, e2_ref,
                       q_scr, a_scr, b_scr):
    h3 = _rms(x_ref[...], g_ref[...]).astype(BF16)
    h_ref[...] = h3
    q = jnp.dot(h3, wq_ref[...], preferred_element_type=F32).astype(BF16)
    for c in range(2 * PEER_HEADS):
        q_scr[c] = q[:, c * PEER_KEYS:(c + 1) * PEER_KEYS]

    def head_body(hd, carry):
        dn = (((1,), (1,)), ((), ()))
        s1_all = lax.dot_general(sk_ref[0], q_scr[2 * hd], dn, preferred_element_type=F32)
        s2_all = lax.dot_general(sk_ref[1], q_scr[2 * hd + 1], dn, preferred_element_type=F32)
        for lc in range(PEER_TR // LANES):
            cols = slice(lc * LANES, (lc + 1) * LANES)
            s1 = s1_all[:, cols]
            s2 = s2_all[:, cols]
            cur = s1
            for r in range(PEER_TOPK):
                m = jnp.max(cur, axis=0, keepdims=True)
                a_scr[r:r + 1, :] = m
                cur = jnp.where(cur >= m, NEG_INF, cur)
            cur = s2
            rank = jnp.full_like(s2, RANK_REST)
            for r in range(PEER_TOPK):
                m = jnp.max(cur, axis=0, keepdims=True)
                b_scr[r:r + 1, :] = m
                hit = cur >= m
                rank = jnp.where(hit, float(r), rank)
                cur = jnp.where(hit, NEG_INF, cur)
            a = a_scr[...]
            b = b_scr[...]
            blocks = [a[0:1, :] + b]
            for r in range(1, 8):
                blocks.append(a[r:r + 1, :] + b[0:8, :])
            blocks.append(a[8:16, :] + b[0:1, :])
            cur = jnp.concatenate(blocks, axis=0)
            top = a[0:1, :] + b[0:1, :]
            zsum = jnp.zeros_like(top)
            m = top
            for r in range(PEER_TOPK):
                m = jnp.max(cur, axis=0, keepdims=True)
                zsum = zsum + jnp.exp(m - top)
                cur = jnp.where(cur >= m, NEG_INF, cur)
            tau = m
            cnt = jnp.zeros_like(s1)
            for c in range(PEER_TOPK):
                cnt = jnp.where(s1 + b[c:c + 1, :] >= tau, float(c + 1), cnt)
            cnt_ref[hd, :, cols] = cnt
            e1_ref[hd, :, cols] = jnp.exp(s1 - a[0:1, :]) * (1.0 / zsum)
            rank_ref[hd, :, cols] = pltpu.bitcast(rank.astype(BF16), jnp.uint32)
            e2_ref[hd, :, cols] = pltpu.bitcast(jnp.exp(s2 - b[0:1, :]).astype(BF16), jnp.uint32)
        return carry

    lax.fori_loop(0, PEER_HEADS, head_body, 0)


def peer_route(x2, g, w_pq, sub_keys):
    n, d = x2.shape
    tr = PEER_TR
    qd = w_pq.shape[1]
    f32_shape = jax.ShapeDtypeStruct((PEER_HEADS, PEER_KEYS, n), F32)
    u32_shape = jax.ShapeDtypeStruct((PEER_HEADS, PEER_KEYS // 2, n), jnp.uint32)
    f32_spec = pl.BlockSpec((PEER_HEADS, PEER_KEYS, tr), lambda i: (0, 0, i))
    u32_spec = pl.BlockSpec((PEER_HEADS, PEER_KEYS // 2, tr), lambda i: (0, 0, i))
    return pl.pallas_call(
        _peer_route_kernel,
        grid=(n // tr,),
        in_specs=[
            pl.BlockSpec((tr, d), lambda i: (i, 0)),
            pl.BlockSpec((1, d), lambda i: (0, 0)),
            pl.BlockSpec((d, qd), lambda i: (0, 0)),
            pl.BlockSpec((2, PEER_KEYS, PEER_KEYS), lambda i: (0, 0, 0)),
        ],
        out_specs=[pl.BlockSpec((tr, d), lambda i: (i, 0)),
                   f32_spec, f32_spec, u32_spec, u32_spec],
        out_shape=[jax.ShapeDtypeStruct((n, d), BF16),
                   f32_shape, f32_shape, u32_shape, u32_shape],
        scratch_shapes=[
            pltpu.VMEM((2 * PEER_HEADS, tr, PEER_KEYS), BF16),
            pltpu.VMEM((PEER_TOPK, LANES), F32),
            pltpu.VMEM((PEER_TOPK, LANES), F32),
        ],
        compiler_params=_cparams(1),
        name="peer_route",
    )(x2, g, w_pq, sub_keys)


PEER_TN = 512
PEER_EB = 1024
PEER_TC = 256


def _gelu_tanh(a):
    inner = math.sqrt(2.0 / math.pi) * (a + 0.044715 * (a * a * a))
    return 0.5 * a * (1.0 + jnp.tanh(inner))


BF16_ROWS = 16


def _peer_dense_kernel(h_ref, x_ref, u_ref, vt_ref, cnt_ref, e1_ref, rank_ref, e2_ref, gf_ref,
                       o_ref, at_scr, g_scr, acc_scr):
    e = pl.program_id(1)
    last = pl.num_programs(1) - 1
    tn = h_ref.shape[0]
    reps = PEER_KEYS // BF16_ROWS

    def row_bf16(ref, hd, ii, cols):
        r16 = jnp.broadcast_to(ref[hd, ii:ii + 1, cols], (BF16_ROWS, LANES)).astype(BF16)
        return jnp.concatenate([r16] * reps, axis=0)

    def scores(c):
        toks = slice(c * PEER_TC, (c + 1) * PEER_TC)
        at_scr[:, toks] = lax.dot_general(u_ref[...], h_ref[toks, :], (((1,), (1,)), ((), ())),
                                          preferred_element_type=F32)

    def gated(c):
        for ii in range(PEER_EB // PEER_KEYS):
            rows = slice(ii * PEER_KEYS, (ii + 1) * PEER_KEYS)
            for lc in range(c * PEER_TC // LANES, (c + 1) * PEER_TC // LANES):
                cols = slice(lc * LANES, (lc + 1) * LANES)
                wgt = jnp.zeros((PEER_KEYS, LANES), BF16)
                for hd in range(PEER_HEADS):
                    rank = pltpu.bitcast(rank_ref[hd, :, cols], BF16)
                    e2 = pltpu.bitcast(e2_ref[hd, :, cols], BF16)
                    kept = jnp.where(rank < row_bf16(cnt_ref, hd, ii, cols), e2,
                                     jnp.zeros((), BF16))
                    wgt = wgt + kept * row_bf16(e1_ref, hd, ii, cols)
                g_scr[rows, cols] = _gelu_tanh(at_scr[rows, cols]).astype(BF16) * wgt

    def combine(c):
        toks = slice(c * PEER_TC, (c + 1) * PEER_TC)
        out_t = jnp.dot(vt_ref[...], g_scr[:, toks], preferred_element_type=F32)
        acc_scr[:, toks] = jnp.where(e == 0, out_t, acc_scr[:, toks] + out_t)

    nch = tn // PEER_TC
    scores(0)
    for c in range(nch):
        if c + 1 < nch:
            scores(c + 1)
        gated(c)
        combine(c)

    @pl.when(e == last)
    def _():
        x3 = x_ref[...] + acc_scr[...].T
        o_ref[...] = _rms(x3, gf_ref[...])


def peer_dense(h3, x2, u, vt, cnt, e1, rank, e2, g_final):
    n, d = x2.shape
    ne = u.shape[0]
    tn, eb = PEER_TN, PEER_EB
    ib = eb // PEER_KEYS
    tok_spec = pl.BlockSpec((PEER_HEADS, PEER_KEYS // 2, tn), lambda i, e: (0, 0, i))
    blk_spec = pl.BlockSpec((PEER_HEADS, ib, tn), lambda i, e: (0, e, i))
    return pl.pallas_call(
        _peer_dense_kernel,
        grid=(n // tn, ne // eb),
        in_specs=[
            pl.BlockSpec((tn, d), lambda i, e: (i, 0)),
            pl.BlockSpec((tn, d), lambda i, e: (i, 0)),
            pl.BlockSpec((eb, d), lambda i, e: (e, 0)),
            pl.BlockSpec((d, eb), lambda i, e: (0, e)),
            blk_spec, blk_spec, tok_spec, tok_spec,
            pl.BlockSpec((1, d), lambda i, e: (0, 0)),
        ],
        out_specs=pl.BlockSpec((tn, d), lambda i, e: (i, 0)),
        out_shape=jax.ShapeDtypeStruct((n, d), F32),
        scratch_shapes=[
            pltpu.VMEM((eb, tn), F32),
            pltpu.VMEM((eb, tn), BF16),
            pltpu.VMEM((d, tn), F32),
        ],
        compiler_params=_cparams(2),
        name="peer_dense",
    )(h3, x2, u, vt, cnt, e1, rank, e2, g_final)


def kernel(x, mem, norm_mix_g, w_in, b_gate, conv_dw_w, conv_dw_b, conv_ln_g, conv_ln_b,
           w_conv_out, w_attn_out, w_out, norm_xattn_g, norm_mem_g, w_xq, w_xkv, w_xo,
           norm_ffn_g, w_peer_q, peer_sub_keys, peer_u, peer_v, norm_final_g):
    batch, seq, d = x.shape
    mem_len = mem.shape[1]
    depth = w_in.shape[0]
    c = conv_dw_w.shape[2]
    d_att = w_attn_out.shape[1]
    n = batch * seq
    xf = x.reshape(n, d)
    memf = mem.reshape(batch * mem_len, d)
    row = lambda v: v.reshape(1, -1).astype(F32)
    q_col = 2 * c // SB_GW
    k_col = q_col + d_att // SB_GW
    v_col = k_col + d_att // SB_GW
    gate_col = (2 * c + 3 * d_att) // d

    for l in range(depth):
        proj = norm_matmul(xf, row(norm_mix_g[l]), w_in[l].astype(BF16), 1024, 1024)
        y = dwconv(proj, conv_dw_w[l], row(conv_dw_b[l]), batch, seq, 512)
        gc = conv_out(y, proj, row(conv_ln_g[l]), row(conv_ln_b[l]), row(b_gate[l, :d]),
                      w_conv_out[l].astype(BF16), gate_col, 512)
        attn = stick_breaking(proj, batch, seq, q_col, k_col, v_col)
        x1 = merge(attn, gc, proj, xf, row(b_gate[l, d:]), w_attn_out[l].astype(BF16),
                   w_out[l].astype(BF16), gate_col + 1, 512)
        kv = norm_matmul(memf, row(norm_mem_g[l]), w_xkv[l].astype(BF16), 256, 1024)
        x2 = cross_attention(x1, row(norm_xattn_g[l]), kv, w_xq[l].astype(BF16),
                             w_xo[l].astype(BF16), seq, mem_len, 512)
        h3, cnt, e1, rank, e2 = peer_route(x2, row(norm_ffn_g[l]), w_peer_q[l].astype(BF16),
                                           peer_sub_keys[l].astype(BF16))
        last = l == depth - 1
        gf = row(norm_final_g) if last else None
        assert last, "the final RMSNorm is fused into the last layer's PEER stage"
        xf = peer_dense(h3, x2, peer_u[l].astype(BF16), peer_v[l].T.astype(BF16),
                        cnt, e1, rank, e2, gf)
    return xf.reshape(batch, seq, d)
```

```python
import functools
import math

import jax
import jax.numpy as jnp
from jax import lax
from jax.experimental import pallas as pl
from jax.experimental.pallas import tpu as pltpu

F32 = jnp.float32
BF16 = jnp.bfloat16

D_MODEL = 1024
CONV_WIDTH = 31
SB_HEADS = 16
SB_HEAD_DIM = 64
XA_HEADS = 4
XA_HEAD_DIM = 256
PEER_HEADS = 8
PEER_KEYS = 128
PEER_TOPK = 16
EPS = 1e-6

LANES = 128
VMEM_LIMIT = 56 * 1024 * 1024

NEG_INF = float("-inf")


def _cparams(n_axes):
    return pltpu.CompilerParams(
        dimension_semantics=("arbitrary",) * n_axes, vmem_limit_bytes=VMEM_LIMIT)


def _rms(x, g):
    ms = jnp.mean(x * x, axis=-1, keepdims=True)
    return x * lax.rsqrt(ms + EPS) * g


def _norm_mm_kernel(x_ref, g_ref, w_ref, o_ref, h_scr):
    @pl.when(pl.program_id(1) == 0)
    def _():
        h_scr[...] = _rms(x_ref[...], g_ref[...]).astype(BF16)

    o_ref[...] = jnp.dot(h_scr[...], w_ref[...], preferred_element_type=F32).astype(o_ref.dtype)


def norm_matmul(x, g, w, tm, tn):
    n, d = x.shape
    m = w.shape[1]
    return pl.pallas_call(
        _norm_mm_kernel,
        grid=(n // tm, m // tn),
        in_specs=[
            pl.BlockSpec((tm, d), lambda i, j: (i, 0)),
            pl.BlockSpec((1, d), lambda i, j: (0, 0)),
            pl.BlockSpec((d, tn), lambda i, j: (0, j)),
        ],
        out_specs=pl.BlockSpec((tm, tn), lambda i, j: (i, j)),
        out_shape=jax.ShapeDtypeStruct((n, m), BF16),
        scratch_shapes=[pltpu.VMEM((tm, d), BF16)],
        compiler_params=_cparams(2),
        name="norm_matmul",
    )(x, g, w)


CONV_HALO = 32
CONV_ROWS = 64


def _dwconv_kernel(a_ref, gt_ref, ah_ref, gh_ref, w_ref, b_ref, y_ref, u_scr):
    ts = a_ref.shape[0]
    first = pl.program_id(1) == 0
    u_scr[CONV_HALO:CONV_HALO + ts, :] = (
        a_ref[...].astype(F32) * jax.nn.sigmoid(gt_ref[...].astype(F32)))
    uh = ah_ref[...].astype(F32) * jax.nn.sigmoid(gh_ref[...].astype(F32))
    u_scr[0:CONV_HALO, :] = jnp.where(first, 0.0, uh)
    off = CONV_HALO - (CONV_WIDTH - 1)
    for r in range(ts // CONV_ROWS):
        r0 = r * CONV_ROWS
        acc = jnp.zeros((CONV_ROWS, LANES), F32) + b_ref[...]
        for k in range(CONV_WIDTH):
            acc = acc + w_ref[k:k + 1, :] * u_scr[r0 + off + k:r0 + off + k + CONV_ROWS, :]
        y_ref[r0:r0 + CONV_ROWS, :] = acc


def dwconv(proj, dw_w, dw_b, batch, seq, ts):
    n = proj.shape[0]
    c = dw_w.shape[1]
    nct = c // LANES
    spt = seq // ts
    hb = ts // CONV_HALO

    def halo_idx(col0):
        def f(b, i, cc):
            return (jnp.maximum((b * spt + i) * hb - 1, 0), col0 + cc)
        return f

    return pl.pallas_call(
        _dwconv_kernel,
        grid=(batch, spt, nct),
        in_specs=[
            pl.BlockSpec((ts, LANES), lambda b, i, cc: (b * spt + i, cc)),
            pl.BlockSpec((ts, LANES), lambda b, i, cc: (b * spt + i, nct + cc)),
            pl.BlockSpec((CONV_HALO, LANES), halo_idx(0)),
            pl.BlockSpec((CONV_HALO, LANES), halo_idx(nct)),
            pl.BlockSpec((CONV_WIDTH, LANES), lambda b, i, cc: (0, cc)),
            pl.BlockSpec((1, LANES), lambda b, i, cc: (0, cc)),
        ],
        out_specs=pl.BlockSpec((ts, LANES), lambda b, i, cc: (b * spt + i, cc)),
        out_shape=jax.ShapeDtypeStruct((n, c), F32),
        scratch_shapes=[pltpu.VMEM((CONV_HALO + ts, LANES), F32)],
        compiler_params=_cparams(3),
        name="dwconv",
    )(proj, proj, proj, proj, dw_w, dw_b)


def _conv_out_kernel(y_ref, gl_ref, lng_ref, lnb_ref, bg_ref, w_ref, o_ref):
    y = y_ref[...]
    mu = jnp.mean(y, axis=-1, keepdims=True)
    yc = y - mu
    var = jnp.mean(yc * yc, axis=-1, keepdims=True)
    yn = yc * lax.rsqrt(var + EPS) * lng_ref[...] + lnb_ref[...]
    act = (yn * jax.nn.sigmoid(yn)).astype(BF16)
    co = jnp.dot(act, w_ref[...], preferred_element_type=F32)
    gate = jax.nn.sigmoid(gl_ref[...].astype(F32) + bg_ref[...])
    o_ref[...] = (gate * co).astype(o_ref.dtype)


def conv_out(y, proj, ln_g, ln_b, bg_conv, w, gate_col_block, tm):
    n, c = y.shape
    d = w.shape[1]
    return pl.pallas_call(
        _conv_out_kernel,
        grid=(n // tm,),
        in_specs=[
            pl.BlockSpec((tm, c), lambda i: (i, 0)),
            pl.BlockSpec((tm, d), lambda i: (i, gate_col_block)),
            pl.BlockSpec((1, c), lambda i: (0, 0)),
            pl.BlockSpec((1, c), lambda i: (0, 0)),
            pl.BlockSpec((1, d), lambda i: (0, 0)),
            pl.BlockSpec((c, d), lambda i: (0, 0)),
        ],
        out_specs=pl.BlockSpec((tm, d), lambda i: (i, 0)),
        out_shape=jax.ShapeDtypeStruct((n, d), BF16),
        compiler_params=_cparams(1),
        name="conv_out",
    )(y, proj, ln_g, ln_b, bg_conv, w)


SB_TQ = 256
SB_TK = 256
SB_CHUNK = 32
SB_GROUP = 4
SB_GW = SB_GROUP * SB_HEAD_DIM


def _softplus(z):
    return jnp.maximum(z, 0.0) + jnp.log(1.0 + jnp.exp2(jnp.abs(z) * (-1.0 / math.log(2.0))))


def _sb_kernel(q_ref, k_ref, v_ref, tri_ref, o_ref, z_scr, sp_scr, w_scr, acc_scr, run_scr):
    qi = pl.program_id(2)
    tq, tk = SB_TQ, SB_TK
    heads = range(SB_GROUP)
    lane = lax.broadcasted_iota(jnp.int32, (1, SB_GW), 1)
    in_head = [(lane >= hd * SB_HEAD_DIM) & (lane < (hd + 1) * SB_HEAD_DIM) for hd in heads]
    qg = q_ref[...]
    scale = jnp.asarray(1.0 / math.sqrt(SB_HEAD_DIM), qg.dtype)
    qh = [jnp.where(in_head[hd], qg, jnp.zeros_like(qg)) * scale for hd in heads]
    chunks = [slice(c * SB_CHUNK, (c + 1) * SB_CHUNK) for c in range(tq // SB_CHUNK)]

    def causal(c):
        qpos = c * SB_CHUNK + lax.broadcasted_iota(jnp.int32, (SB_CHUNK, tk), 0)
        kpos = lax.broadcasted_iota(jnp.int32, (SB_CHUNK, tk), 1)
        return kpos < qpos

    def region(js, diags):
        slots = range(len(js))
        kbs = [k_ref[pl.ds(pl.multiple_of(j * tk, tk), tk), :] for j in js]
        vbs = [v_ref[pl.ds(pl.multiple_of(j * tk, tk), tk), :] for j in js]
        for s in slots:
            for hd in heads:
                z_scr[hd, s] = lax.dot_general(qh[hd], kbs[s], (((1,), (1,)), ((), ())),
                                               preferred_element_type=F32)
        for s in slots:
            for hd in heads:
                for c, rows in enumerate(chunks):
                    sp = _softplus(z_scr[hd, s, rows, :])
                    if diags[s]:
                        sp = jnp.where(causal(c), sp, 0.0)
                    sp_scr[hd, s, rows, :] = sp.astype(BF16)
        for s in slots:
            for hd in heads:
                inc = jnp.dot(sp_scr[hd, s], tri_ref[...], preferred_element_type=F32)
                for c, rows in enumerate(chunks):
                    logw = z_scr[hd, s, rows, :] + inc[rows, :]
                    if diags[s]:
                        w = jnp.where(causal(c), jnp.exp(logw), 0.0)
                        run_scr[hd, rows, :] = inc[rows, 0:1]
                    else:
                        run = run_scr[hd, rows, :]
                        w = jnp.exp(logw + run)
                        run_scr[hd, rows, :] = run + inc[rows, 0:1]
                    w_scr[hd, s, rows, :] = w.astype(BF16)
        for hd in heads:
            for s in slots:
                pv = jnp.dot(w_scr[hd, s], vbs[s], preferred_element_type=F32)
                if diags[s]:
                    acc_scr[hd] = pv
                else:
                    acc_scr[hd] += pv

    odd = qi % 2

    @pl.when(odd == 0)
    def _():
        region([qi], [True])

    @pl.when(odd == 1)
    def _():
        region([qi, qi - 1], [True, False])

    def body(p, carry):
        j = qi - odd - 1 - 2 * p
        region([j, j - 1], [False, False])
        return carry

    lax.fori_loop(0, (qi - odd) // 2, body, 0)
    out = acc_scr[0]
    for hd in range(1, SB_GROUP):
        out = jnp.where(in_head[hd], acc_scr[hd], out)
    o_ref[...] = out.astype(o_ref.dtype)


def _neg_tri(tk):
    j = lax.broadcasted_iota(jnp.int32, (tk, tk), 0)
    s = lax.broadcasted_iota(jnp.int32, (tk, tk), 1)
    return jnp.where(j >= s, -1.0, 0.0).astype(BF16)


def stick_breaking(proj, batch, seq, q_col, k_col, v_col):
    n = proj.shape[0]
    nq = seq // SB_TQ
    ngroup = SB_HEADS // SB_GROUP
    tri = _neg_tri(SB_TK)
    return pl.pallas_call(
        _sb_kernel,
        grid=(batch, ngroup, nq),
        in_specs=[
            pl.BlockSpec((SB_TQ, SB_GW), lambda b, g, qi: (b * nq + qi, q_col + g)),
            pl.BlockSpec((seq, SB_GW), lambda b, g, qi: (b, k_col + g)),
            pl.BlockSpec((seq, SB_GW), lambda b, g, qi: (b, v_col + g)),
            pl.BlockSpec((SB_TK, SB_TK), lambda b, g, qi: (0, 0)),
        ],
        out_specs=pl.BlockSpec((SB_TQ, SB_GW), lambda b, g, qi: (b * nq + qi, g)),
        out_shape=jax.ShapeDtypeStruct((n, SB_HEADS * SB_HEAD_DIM), BF16),
        scratch_shapes=[
            pltpu.VMEM((SB_GROUP, 2, SB_TQ, SB_TK), F32),
            pltpu.VMEM((SB_GROUP, 2, SB_TQ, SB_TK), BF16),
            pltpu.VMEM((SB_GROUP, 2, SB_TQ, SB_TK), BF16),
            pltpu.VMEM((SB_GROUP, SB_TQ, SB_GW), F32),
            pltpu.VMEM((SB_GROUP, SB_TQ, 1), F32),
        ],
        compiler_params=_cparams(3),
        name="stick_breaking",
    )(proj, proj, proj, tri)


def _merge_kernel(attn_ref, gc_ref, gl_ref, x_ref, bg_ref, wa_ref, wo_ref, o_ref):
    ao = jnp.dot(attn_ref[...], wa_ref[...], preferred_element_type=F32)
    gate = jax.nn.sigmoid(gl_ref[...].astype(F32) + bg_ref[...])
    mixed = (gc_ref[...].astype(F32) + gate * ao).astype(BF16)
    o_ref[...] = x_ref[...] + jnp.dot(mixed, wo_ref[...], preferred_element_type=F32)


def merge(attn, gc, proj, x, bg_attn, w_attn_out, w_out, gate_col_block, tm):
    n, d = x.shape
    da = attn.shape[1]
    return pl.pallas_call(
        _merge_kernel,
        grid=(n // tm,),
        in_specs=[
            pl.BlockSpec((tm, da), lambda i: (i, 0)),
            pl.BlockSpec((tm, d), lambda i: (i, 0)),
            pl.BlockSpec((tm, d), lambda i: (i, gate_col_block)),
            pl.BlockSpec((tm, d), lambda i: (i, 0)),
            pl.BlockSpec((1, d), lambda i: (0, 0)),
            pl.BlockSpec((da, d), lambda i: (0, 0)),
            pl.BlockSpec((d, d), lambda i: (0, 0)),
        ],
        out_specs=pl.BlockSpec((tm, d), lambda i: (i, 0)),
        out_shape=jax.ShapeDtypeStruct((n, d), F32),
        compiler_params=_cparams(1),
        name="merge",
    )(attn, gc, proj, x, bg_attn, w_attn_out, w_out)


def _xattn_kernel(x_ref, g_ref, kv_ref, wq_ref, wo_ref, o_ref):
    x = x_ref[...]
    h = _rms(x, g_ref[...]).astype(BF16)
    q = jnp.dot(h, wq_ref[...], preferred_element_type=F32)
    q = (q * (1.0 / math.sqrt(XA_HEAD_DIM))).astype(BF16)
    d = XA_HEADS * XA_HEAD_DIM
    outs = []
    for hd in range(XA_HEADS):
        cols = slice(hd * XA_HEAD_DIM, (hd + 1) * XA_HEAD_DIM)
        kh = kv_ref[:, cols]
        vh = kv_ref[:, d + hd * XA_HEAD_DIM:d + (hd + 1) * XA_HEAD_DIM]
        sc = lax.dot_general(q[:, cols], kh, (((1,), (1,)), ((), ())),
                             preferred_element_type=F32)
        p = jnp.exp(sc - jnp.max(sc, axis=-1, keepdims=True))
        p = (p / jnp.sum(p, axis=-1, keepdims=True)).astype(BF16)
        outs.append(jnp.dot(p, vh, preferred_element_type=F32).astype(BF16))
    o = jnp.concatenate(outs, axis=1)
    o_ref[...] = x + jnp.dot(o, wo_ref[...], preferred_element_type=F32)


def cross_attention(x1, g, kv, w_xq, w_xo, seq, mem_len, tm):
    n, d = x1.shape
    tiles_per_batch = seq // tm
    return pl.pallas_call(
        _xattn_kernel,
        grid=(n // tm,),
        in_specs=[
            pl.BlockSpec((tm, d), lambda i: (i, 0)),
            pl.BlockSpec((1, d), lambda i: (0, 0)),
            pl.BlockSpec((mem_len, 2 * d), lambda i: (i // tiles_per_batch, 0)),
            pl.BlockSpec((d, d), lambda i: (0, 0)),
            pl.BlockSpec((d, d), lambda i: (0, 0)),
        ],
        out_specs=pl.BlockSpec((tm, d), lambda i: (i, 0)),
        out_shape=jax.ShapeDtypeStruct((n, d), F32),
        compiler_params=_cparams(1),
        name="cross_attention",
    )(x1, g, kv, w_xq, w_xo)


PEER_TR = 256
RANK_REST = 127.0


def _peer_route_kernel(x_ref, g_ref, wq_ref, sk_ref, h_ref, cnt_ref, e1_ref, rank_ref, e2_ref,
                       q_scr, a_scr, b_scr):
    h3 = _rms(x_ref[...], g_ref[...]).astype(BF16)
    h_ref[...] = h3
    q = jnp.dot(h3, wq_ref[...], preferred_element_type=F32).astype(BF16)
    for c in range(2 * PEER_HEADS):
        q_scr[c] = q[:, c * PEER_KEYS:(c + 1) * PEER_KEYS]

    def head_body(hd, carry):
        dn = (((1,), (1,)), ((), ()))
        s1_all = lax.dot_general(sk_ref[0], q_scr[2 * hd], dn, preferred_element_type=F32)
        s2_all = lax.dot_general(sk_ref[1], q_scr[2 * hd + 1], dn, preferred_element_type=F32)
        for lc in range(PEER_TR // LANES):
            cols = slice(lc * LANES, (lc + 1) * LANES)
            s1 = s1_all[:, cols]
            s2 = s2_all[:, cols]
            cur = s1
            for r in range(PEER_TOPK):
                m = jnp.max(cur, axis=0, keepdims=True)
                a_scr[r:r + 1, :] = m
                cur = jnp.where(cur >= m, NEG_INF, cur)
            cur = s2
            rank = jnp.full_like(s2, RANK_REST)
            for r in range(PEER_TOPK):
                m = jnp.max(cur, axis=0, keepdims=True)
                b_scr[r:r + 1, :] = m
                hit = cur >= m
                rank = jnp.where(hit, float(r), rank)
                cur = jnp.where(hit, NEG_INF, cur)
            a = a_scr[...]
            b = b_scr[...]
            blocks = [a[0:1, :] + b]
            for r in range(1, 8):
                blocks.append(a[r:r + 1, :] + b[0:8, :])
            blocks.append(a[8:16, :] + b[0:1, :])
            cur = jnp.concatenate(blocks, axis=0)
            top = a[0:1, :] + b[0:1, :]
            zsum = jnp.zeros_like(top)
            m = top
            for r in range(PEER_TOPK):
                m = jnp.max(cur, axis=0, keepdims=True)
                zsum = zsum + jnp.exp(m - top)
                cur = jnp.where(cur >= m, NEG_INF, cur)
            tau = m
            cnt = jnp.zeros_like(s1)
            for c in range(PEER_TOPK):
                cnt = jnp.where(s1 + b[c:c + 1, :] >= tau, float(c + 1), cnt)
            cnt_ref[hd, :, cols] = cnt
            e1_ref[hd, :, cols] = jnp.exp(s1 - a[0:1, :]) * (1.0 / zsum)
            rank_ref[hd, :, cols] = pltpu.bitcast(rank.astype(BF16), jnp.uint32)
            e2_ref[hd, :, cols] = pltpu.bitcast(jnp.exp(s2 - b[0:1, :]).astype(BF16), jnp.uint32)
        return carry

    lax.fori_loop(0, PEER_HEADS, head_body, 0)


def peer_route(x2, g, w_pq, sub_keys):
    n, d = x2.shape
    tr = PEER_TR
    qd = w_pq.shape[1]
    f32_shape = jax.ShapeDtypeStruct((PEER_HEADS, PEER_KEYS, n), F32)
    u32_shape = jax.ShapeDtypeStruct((PEER_HEADS, PEER_KEYS // 2, n), jnp.uint32)
    f32_spec = pl.BlockSpec((PEER_HEADS, PEER_KEYS, tr), lambda i: (0, 0, i))
    u32_spec = pl.BlockSpec((PEER_HEADS, PEER_KEYS // 2, tr), lambda i: (0, 0, i))
    return pl.pallas_call(
        _peer_route_kernel,
        grid=(n // tr,),
        in_specs=[
            pl.BlockSpec((tr, d), lambda i: (i, 0)),
            pl.BlockSpec((1, d), lambda i: (0, 0)),
            pl.BlockSpec((d, qd), lambda i: (0, 0)),
            pl.BlockSpec((2, PEER_KEYS, PEER_KEYS), lambda i: (0, 0, 0)),
        ],
        out_specs=[pl.BlockSpec((tr, d), lambda i: (i, 0)),
                   f32_spec, f32_spec, u32_spec, u32_spec],
        out_shape=[jax.ShapeDtypeStruct((n, d), BF16),
                   f32_shape, f32_shape, u32_shape, u32_shape],
        scratch_shapes=[
            pltpu.VMEM((2 * PEER_HEADS, tr, PEER_KEYS), BF16),
            pltpu.VMEM((PEER_TOPK, LANES), F32),
            pltpu.VMEM((PEER_TOPK, LANES), F32),
        ],
        compiler_params=_cparams(1),
        name="peer_route",
    )(x2, g, w_pq, sub_keys)


PEER_TN = 512
PEER_EB = 1024
PEER_TC = 256


def _gelu_tanh(a):
    inner = math.sqrt(2.0 / math.pi) * (a + 0.044715 * (a * a * a))
    return 0.5 * a * (1.0 + jnp.tanh(inner))


BF16_ROWS = 16


def _peer_dense_kernel(h_ref, x_ref, u_ref, vt_ref, cnt_ref, e1_ref, rank_ref, e2_ref, gf_ref,
                       o_ref, at_scr, g_scr, acc_scr):
    e = pl.program_id(1)
    last = pl.num_programs(1) - 1
    tn = h_ref.shape[0]
    reps = PEER_KEYS // BF16_ROWS

    def row_bf16(ref, hd, ii, cols):
        r16 = jnp.broadcast_to(ref[hd, ii:ii + 1, cols], (BF16_ROWS, LANES)).astype(BF16)
        return jnp.concatenate([r16] * reps, axis=0)

    def scores(c):
        toks = slice(c * PEER_TC, (c + 1) * PEER_TC)
        at_scr[:, toks] = lax.dot_general(u_ref[...], h_ref[toks, :], (((1,), (1,)), ((), ())),
                                          preferred_element_type=F32)

    def gated(c):
        for ii in range(PEER_EB // PEER_KEYS):
            rows = slice(ii * PEER_KEYS, (ii + 1) * PEER_KEYS)
            for lc in range(c * PEER_TC // LANES, (c + 1) * PEER_TC // LANES):
                cols = slice(lc * LANES, (lc + 1) * LANES)
                wgt = jnp.zeros((PEER_KEYS, LANES), BF16)
                for hd in range(PEER_HEADS):
                    rank = pltpu.bitcast(rank_ref[hd, :, cols], BF16)
                    e2 = pltpu.bitcast(e2_ref[hd, :, cols], BF16)
                    kept = jnp.where(rank < row_bf16(cnt_ref, hd, ii, cols), e2,
                                     jnp.zeros((), BF16))
                    wgt = wgt + kept * row_bf16(e1_ref, hd, ii, cols)
                g_scr[rows, cols] = _gelu_tanh(at_scr[rows, cols].astype(BF16)) * wgt

    def combine(c):
        toks = slice(c * PEER_TC, (c + 1) * PEER_TC)
        out_t = jnp.dot(vt_ref[...], g_scr[:, toks], preferred_element_type=F32)
        acc_scr[:, toks] = jnp.where(e == 0, out_t, acc_scr[:, toks] + out_t)

    nch = tn // PEER_TC
    scores(0)
    for c in range(nch):
        if c + 1 < nch:
            scores(c + 1)
        gated(c)
        combine(c)

    @pl.when(e == last)
    def _():
        x3 = x_ref[...] + acc_scr[...].T
        o_ref[...] = _rms(x3, gf_ref[...])


def peer_dense(h3, x2, u, vt, cnt, e1, rank, e2, g_final):
    n, d = x2.shape
    ne = u.shape[0]
    tn, eb = PEER_TN, PEER_EB
    ib = eb // PEER_KEYS
    tok_spec = pl.BlockSpec((PEER_HEADS, PEER_KEYS // 2, tn), lambda i, e: (0, 0, i))
    blk_spec = pl.BlockSpec((PEER_HEADS, ib, tn), lambda i, e: (0, e, i))
    return pl.pallas_call(
        _peer_dense_kernel,
        grid=(n // tn, ne // eb),
        in_specs=[
            pl.BlockSpec((tn, d), lambda i, e: (i, 0)),
            pl.BlockSpec((tn, d), lambda i, e: (i, 0)),
            pl.BlockSpec((eb, d), lambda i, e: (e, 0)),
            pl.BlockSpec((d, eb), lambda i, e: (0, e)),
            blk_spec, blk_spec, tok_spec, tok_spec,
            pl.BlockSpec((1, d), lambda i, e: (0, 0)),
        ],
        out_specs=pl.BlockSpec((tn, d), lambda i, e: (i, 0)),
        out_shape=jax.ShapeDtypeStruct((n, d), F32),
        scratch_shapes=[
            pltpu.VMEM((eb, tn), F32),
            pltpu.VMEM((eb, tn), BF16),
            pltpu.VMEM((d, tn), F32),
        ],
        compiler_params=_cparams(2),
        name="peer_dense",
    )(h3, x2, u, vt, cnt, e1, rank, e2, g_final)


def kernel(x, mem, norm_mix_g, w_in, b_gate, conv_dw_w, conv_dw_b, conv_ln_g, conv_ln_b,
           w_conv_out, w_attn_out, w_out, norm_xattn_g, norm_mem_g, w_xq, w_xkv, w_xo,
           norm_ffn_g, w_peer_q, peer_sub_keys, peer_u, peer_v, norm_final_g):
    batch, seq, d = x.shape
    mem_len = mem.shape[1]
    depth = w_in.shape[0]
    c = conv_dw_w.shape[2]
    d_att = w_attn_out.shape[1]
    n = batch * seq
    xf = x.reshape(n, d)
    memf = mem.reshape(batch * mem_len, d)
    row = lambda v: v.reshape(1, -1).astype(F32)
    q_col = 2 * c // SB_GW
    k_col = q_col + d_att // SB_GW
    v_col = k_col + d_att // SB_GW
    gate_col = (2 * c + 3 * d_att) // d

    for l in range(depth):
        proj = norm_matmul(xf, row(norm_mix_g[l]), w_in[l].astype(BF16), 1024, 1024)
        y = dwconv(proj, conv_dw_w[l], row(conv_dw_b[l]), batch, seq, 512)
        gc = conv_out(y, proj, row(conv_ln_g[l]), row(conv_ln_b[l]), row(b_gate[l, :d]),
                      w_conv_out[l].astype(BF16), gate_col, 512)
        attn = stick_breaking(proj, batch, seq, q_col, k_col, v_col)
        x1 = merge(attn, gc, proj, xf, row(b_gate[l, d:]), w_attn_out[l].astype(BF16),
                   w_out[l].astype(BF16), gate_col + 1, 512)
        kv = norm_matmul(memf, row(norm_mem_g[l]), w_xkv[l].astype(BF16), 256, 1024)
        x2 = cross_attention(x1, row(norm_xattn_g[l]), kv, w_xq[l].astype(BF16),
                             w_xo[l].astype(BF16), seq, mem_len, 512)
        h3, cnt, e1, rank, e2 = peer_route(x2, row(norm_ffn_g[l]), w_peer_q[l].astype(BF16),
                                           peer_sub_keys[l].astype(BF16))
        last = l == depth - 1
        gf = row(norm_final_g) if last else None
        assert last, "the final RMSNorm is fused into the last layer's PEER stage"
        xf = peer_dense(h3, x2, peer_u[l].astype(BF16), peer_v[l].astype(BF16).T,
                        cnt, e1, rank, e2, gf)
    return xf.reshape(batch, seq, d)
```

```python
import functools
import math

import jax
import jax.numpy as jnp
from jax import lax
from jax.experimental import pallas as pl
from jax.experimental.pallas import tpu as pltpu

F32 = jnp.float32
BF16 = jnp.bfloat16

D_MODEL = 1024
CONV_WIDTH = 31
SB_HEADS = 16
SB_HEAD_DIM = 64
XA_HEADS = 4
XA_HEAD_DIM = 256
PEER_HEADS = 8
PEER_KEYS = 128
PEER_TOPK = 16
EPS = 1e-6

LANES = 128
VMEM_LIMIT = 56 * 1024 * 1024

NEG_INF = float("-inf")


def _cparams(n_axes):
    return pltpu.CompilerParams(
        dimension_semantics=("arbitrary",) * n_axes, vmem_limit_bytes=VMEM_LIMIT)


def _rms(x, g):
    ms = jnp.mean(x * x, axis=-1, keepdims=True)
    return x * lax.rsqrt(ms + EPS) * g


def _norm_mm_kernel(x_ref, g_ref, w_ref, o_ref, h_scr):
    @pl.when(pl.program_id(1) == 0)
    def _():
        h_scr[...] = _rms(x_ref[...], g_ref[...]).astype(BF16)

    o_ref[...] = jnp.dot(h_scr[...], w_ref[...], preferred_element_type=F32).astype(o_ref.dtype)


def norm_matmul(x, g, w, tm, tn):
    n, d = x.shape
    m = w.shape[1]
    return pl.pallas_call(
        _norm_mm_kernel,
        grid=(n // tm, m // tn),
        in_specs=[
            pl.BlockSpec((tm, d), lambda i, j: (i, 0)),
            pl.BlockSpec((1, d), lambda i, j: (0, 0)),
            pl.BlockSpec((d, tn), lambda i, j: (0, j)),
        ],
        out_specs=pl.BlockSpec((tm, tn), lambda i, j: (i, j)),
        out_shape=jax.ShapeDtypeStruct((n, m), BF16),
        scratch_shapes=[pltpu.VMEM((tm, d), BF16)],
        compiler_params=_cparams(2),
        name="norm_matmul",
    )(x, g, w)


CONV_HALO = 32
CONV_ROWS = 64
CONV_LANES = 128


def _dwconv_kernel(a_ref, gt_ref, ah_ref, gh_ref, w_ref, b_ref, y_ref, u_scr):
    ts = a_ref.shape[0]
    first = pl.program_id(1) == 0
    u_scr[CONV_HALO:CONV_HALO + ts, :] = (
        a_ref[...].astype(F32) * jax.nn.sigmoid(gt_ref[...].astype(F32)))
    uh = ah_ref[...].astype(F32) * jax.nn.sigmoid(gh_ref[...].astype(F32))
    u_scr[0:CONV_HALO, :] = jnp.where(first, 0.0, uh)
    off = CONV_HALO - (CONV_WIDTH - 1)
    for lt in range(CONV_LANES // LANES):
        cols = slice(lt * LANES, (lt + 1) * LANES)
        for r in range(ts // CONV_ROWS):
            r0 = r * CONV_ROWS
            acc = jnp.zeros((CONV_ROWS, LANES), F32) + b_ref[:, cols]
            for k in range(CONV_WIDTH):
                acc = acc + w_ref[k:k + 1, cols] * u_scr[r0 + off + k:r0 + off + k + CONV_ROWS, cols]
            y_ref[r0:r0 + CONV_ROWS, cols] = acc


def dwconv(proj, dw_w, dw_b, batch, seq, ts):
    n = proj.shape[0]
    c = dw_w.shape[1]
    nct = c // CONV_LANES
    spt = seq // ts
    hb = ts // CONV_HALO

    def halo_idx(col0):
        def f(b, i, cc):
            return (jnp.maximum((b * spt + i) * hb - 1, 0), col0 + cc)
        return f

    return pl.pallas_call(
        _dwconv_kernel,
        grid=(batch, spt, nct),
        in_specs=[
            pl.BlockSpec((ts, CONV_LANES), lambda b, i, cc: (b * spt + i, cc)),
            pl.BlockSpec((ts, CONV_LANES), lambda b, i, cc: (b * spt + i, nct + cc)),
            pl.BlockSpec((CONV_HALO, CONV_LANES), halo_idx(0)),
            pl.BlockSpec((CONV_HALO, CONV_LANES), halo_idx(nct)),
            pl.BlockSpec((CONV_WIDTH, CONV_LANES), lambda b, i, cc: (0, cc)),
            pl.BlockSpec((1, CONV_LANES), lambda b, i, cc: (0, cc)),
        ],
        out_specs=pl.BlockSpec((ts, CONV_LANES), lambda b, i, cc: (b * spt + i, cc)),
        out_shape=jax.ShapeDtypeStruct((n, c), F32),
        scratch_shapes=[pltpu.VMEM((CONV_HALO + ts, CONV_LANES), F32)],
        compiler_params=_cparams(3),
        name="dwconv",
    )(proj, proj, proj, proj, dw_w, dw_b)


def _conv_out_kernel(y_ref, gl_ref, lng_ref, lnb_ref, bg_ref, w_ref, o_ref):
    y = y_ref[...]
    mu = jnp.mean(y, axis=-1, keepdims=True)
    yc = y - mu
    var = jnp.mean(yc * yc, axis=-1, keepdims=True)
    yn = yc * lax.rsqrt(var + EPS) * lng_ref[...] + lnb_ref[...]
    act = (yn * jax.nn.sigmoid(yn)).astype(BF16)
    co = jnp.dot(act, w_ref[...], preferred_element_type=F32)
    gate = jax.nn.sigmoid(gl_ref[...].astype(F32) + bg_ref[...])
    o_ref[...] = (gate * co).astype(o_ref.dtype)


def conv_out(y, proj, ln_g, ln_b, bg_conv, w, gate_col_block, tm):
    n, c = y.shape
    d = w.shape[1]
    return pl.pallas_call(
        _conv_out_kernel,
        grid=(n // tm,),
        in_specs=[
            pl.BlockSpec((tm, c), lambda i: (i, 0)),
            pl.BlockSpec((tm, d), lambda i: (i, gate_col_block)),
            pl.BlockSpec((1, c), lambda i: (0, 0)),
            pl.BlockSpec((1, c), lambda i: (0, 0)),
            pl.BlockSpec((1, d), lambda i: (0, 0)),
            pl.BlockSpec((c, d), lambda i: (0, 0)),
        ],
        out_specs=pl.BlockSpec((tm, d), lambda i: (i, 0)),
        out_shape=jax.ShapeDtypeStruct((n, d), BF16),
        compiler_params=_cparams(1),
        name="conv_out",
    )(y, proj, ln_g, ln_b, bg_conv, w)


SB_TQ = 256
SB_TK = 256
SB_CHUNK = 32
SB_GROUP = 4
SB_REGION = 2
SB_GW = SB_GROUP * SB_HEAD_DIM


SOFTPLUS_CLAMP = 60.0


def _softplus(z):
    return jnp.maximum(z, jnp.log(1.0 + jnp.exp(jnp.minimum(z, SOFTPLUS_CLAMP))))


def _sb_kernel(q_ref, k_ref, v_ref, tri_ref, o_ref, z_scr, sp_scr, w_scr, acc_scr, run_scr):
    qi = pl.program_id(2)
    tq, tk = SB_TQ, SB_TK
    heads = range(SB_GROUP)
    lane = lax.broadcasted_iota(jnp.int32, (1, SB_GW), 1)
    in_head = [(lane >= hd * SB_HEAD_DIM) & (lane < (hd + 1) * SB_HEAD_DIM) for hd in heads]
    qg = q_ref[...]
    scale = jnp.asarray(1.0 / math.sqrt(SB_HEAD_DIM), qg.dtype)
    qh = [jnp.where(in_head[hd], qg, jnp.zeros_like(qg)) * scale for hd in heads]
    chunks = [slice(c * SB_CHUNK, (c + 1) * SB_CHUNK) for c in range(tq // SB_CHUNK)]

    def causal(c):
        qpos = c * SB_CHUNK + lax.broadcasted_iota(jnp.int32, (SB_CHUNK, tk), 0)
        kpos = lax.broadcasted_iota(jnp.int32, (SB_CHUNK, tk), 1)
        return kpos < qpos

    def region(js, diags):
        slots = range(len(js))
        kbs = [k_ref[pl.ds(pl.multiple_of(j * tk, tk), tk), :] for j in js]
        vbs = [v_ref[pl.ds(pl.multiple_of(j * tk, tk), tk), :] for j in js]
        for s in slots:
            for hd in heads:
                z_scr[hd, s] = lax.dot_general(qh[hd], kbs[s], (((1,), (1,)), ((), ())),
                                               preferred_element_type=F32)
        for s in slots:
            for hd in heads:
                for c, rows in enumerate(chunks):
                    sp = _softplus(z_scr[hd, s, rows, :])
                    if diags[s]:
                        sp = jnp.where(causal(c), sp, 0.0)
                    sp_scr[hd, s, rows, :] = sp.astype(BF16)
        for s in slots:
            for hd in heads:
                inc = jnp.dot(sp_scr[hd, s], tri_ref[...], preferred_element_type=F32)
                for c, rows in enumerate(chunks):
                    logw = z_scr[hd, s, rows, :] + inc[rows, :]
                    if diags[s]:
                        w = jnp.where(causal(c), jnp.exp(logw), 0.0)
                        run_scr[hd, rows, :] = inc[rows, 0:1]
                    else:
                        run = run_scr[hd, rows, :]
                        w = jnp.exp(logw + run)
                        run_scr[hd, rows, :] = run + inc[rows, 0:1]
                    w_scr[hd, s, rows, :] = w.astype(BF16)
        for hd in heads:
            for s in slots:
                pv = jnp.dot(w_scr[hd, s], vbs[s], preferred_element_type=F32)
                if diags[s]:
                    acc_scr[hd] = pv
                else:
                    acc_scr[hd] += pv

    first = qi % SB_REGION + 1
    for nb in range(1, SB_REGION + 1):
        @pl.when(first == nb)
        def _(nb=nb):
            region([qi - t for t in range(nb)], [True] + [False] * (nb - 1))

    def body(p, carry):
        j = qi - first - SB_REGION * p
        region([j - t for t in range(SB_REGION)], [False] * SB_REGION)
        return carry

    lax.fori_loop(0, (qi + 1 - first) // SB_REGION, body, 0)
    out = acc_scr[0]
    for hd in range(1, SB_GROUP):
        out = jnp.where(in_head[hd], acc_scr[hd], out)
    o_ref[...] = out.astype(o_ref.dtype)


def _neg_tri(tk):
    j = lax.broadcasted_iota(jnp.int32, (tk, tk), 0)
    s = lax.broadcasted_iota(jnp.int32, (tk, tk), 1)
    return jnp.where(j >= s, -1.0, 0.0).astype(BF16)


def stick_breaking(proj, batch, seq, q_col, k_col, v_col):
    n = proj.shape[0]
    nq = seq // SB_TQ
    ngroup = SB_HEADS // SB_GROUP
    tri = _neg_tri(SB_TK)
    return pl.pallas_call(
        _sb_kernel,
        grid=(batch, ngroup, nq),
        in_specs=[
            pl.BlockSpec((SB_TQ, SB_GW), lambda b, g, qi: (b * nq + qi, q_col + g)),
            pl.BlockSpec((seq, SB_GW), lambda b, g, qi: (b, k_col + g)),
            pl.BlockSpec((seq, SB_GW), lambda b, g, qi: (b, v_col + g)),
            pl.BlockSpec((SB_TK, SB_TK), lambda b, g, qi: (0, 0)),
        ],
        out_specs=pl.BlockSpec((SB_TQ, SB_GW), lambda b, g, qi: (b * nq + qi, g)),
        out_shape=jax.ShapeDtypeStruct((n, SB_HEADS * SB_HEAD_DIM), BF16),
        scratch_shapes=[
            pltpu.VMEM((SB_GROUP, SB_REGION, SB_TQ, SB_TK), F32),
            pltpu.VMEM((SB_GROUP, SB_REGION, SB_TQ, SB_TK), BF16),
            pltpu.VMEM((SB_GROUP, SB_REGION, SB_TQ, SB_TK), BF16),
            pltpu.VMEM((SB_GROUP, SB_TQ, SB_GW), F32),
            pltpu.VMEM((SB_GROUP, SB_TQ, 1), F32),
        ],
        compiler_params=_cparams(3),
        name="stick_breaking",
    )(proj, proj, proj, tri)


def _merge_kernel(attn_ref, gc_ref, gl_ref, x_ref, bg_ref, wa_ref, wo_ref, o_ref):
    ao = jnp.dot(attn_ref[...], wa_ref[...], preferred_element_type=F32)
    gate = jax.nn.sigmoid(gl_ref[...].astype(F32) + bg_ref[...])
    mixed = (gc_ref[...].astype(F32) + gate * ao).astype(BF16)
    o_ref[...] = x_ref[...] + jnp.dot(mixed, wo_ref[...], preferred_element_type=F32)


def merge(attn, gc, proj, x, bg_attn, w_attn_out, w_out, gate_col_block, tm):
    n, d = x.shape
    da = attn.shape[1]
    return pl.pallas_call(
        _merge_kernel,
        grid=(n // tm,),
        in_specs=[
            pl.BlockSpec((tm, da), lambda i: (i, 0)),
            pl.BlockSpec((tm, d), lambda i: (i, 0)),
            pl.BlockSpec((tm, d), lambda i: (i, gate_col_block)),
            pl.BlockSpec((tm, d), lambda i: (i, 0)),
            pl.BlockSpec((1, d), lambda i: (0, 0)),
            pl.BlockSpec((da, d), lambda i: (0, 0)),
            pl.BlockSpec((d, d), lambda i: (0, 0)),
        ],
        out_specs=pl.BlockSpec((tm, d), lambda i: (i, 0)),
        out_shape=jax.ShapeDtypeStruct((n, d), F32),
        compiler_params=_cparams(1),
        name="merge",
    )(attn, gc, proj, x, bg_attn, w_attn_out, w_out)


def _xattn_kernel(x_ref, g_ref, kv_ref, wq_ref, wo_ref, o_ref):
    x = x_ref[...]
    h = _rms(x, g_ref[...]).astype(BF16)
    q = jnp.dot(h, wq_ref[...], preferred_element_type=F32)
    q = (q * (1.0 / math.sqrt(XA_HEAD_DIM))).astype(BF16)
    d = XA_HEADS * XA_HEAD_DIM
    outs = []
    for hd in range(XA_HEADS):
        cols = slice(hd * XA_HEAD_DIM, (hd + 1) * XA_HEAD_DIM)
        kh = kv_ref[:, cols]
        vh = kv_ref[:, d + hd * XA_HEAD_DIM:d + (hd + 1) * XA_HEAD_DIM]
        sc = lax.dot_general(q[:, cols], kh, (((1,), (1,)), ((), ())),
                             preferred_element_type=F32)
        p = jnp.exp(sc - jnp.max(sc, axis=-1, keepdims=True))
        p = (p / jnp.sum(p, axis=-1, keepdims=True)).astype(BF16)
        outs.append(jnp.dot(p, vh, preferred_element_type=F32).astype(BF16))
    o = jnp.concatenate(outs, axis=1)
    o_ref[...] = x + jnp.dot(o, wo_ref[...], preferred_element_type=F32)


def cross_attention(x1, g, kv, w_xq, w_xo, seq, mem_len, tm):
    n, d = x1.shape
    tiles_per_batch = seq // tm
    return pl.pallas_call(
        _xattn_kernel,
        grid=(n // tm,),
        in_specs=[
            pl.BlockSpec((tm, d), lambda i: (i, 0)),
            pl.BlockSpec((1, d), lambda i: (0, 0)),
            pl.BlockSpec((mem_len, 2 * d), lambda i: (i // tiles_per_batch, 0)),
            pl.BlockSpec((d, d), lambda i: (0, 0)),
            pl.BlockSpec((d, d), lambda i: (0, 0)),
        ],
        out_specs=pl.BlockSpec((tm, d), lambda i: (i, 0)),
        out_shape=jax.ShapeDtypeStruct((n, d), F32),
        compiler_params=_cparams(1),
        name="cross_attention",
    )(x1, g, kv, w_xq, w_xo)


PEER_TR = 256
RANK_REST = 127.0


def _peer_route_kernel(x_ref, g_ref, wq_ref, sk_ref, h_ref, cnt_ref, e1_ref, rank_ref, e2_ref,
                       q_scr, a_scr, b_scr):
    h3 = _rms(x_ref[...], g_ref[...]).astype(BF16)
    h_ref[...] = h3
    q = jnp.dot(h3, wq_ref[...], preferred_element_type=F32).astype(BF16)
    for c in range(2 * PEER_HEADS):
        q_scr[c] = q[:, c * PEER_KEYS:(c + 1) * PEER_KEYS]

    def head_body(hd, carry):
        dn = (((1,), (1,)), ((), ()))
        s1_all = lax.dot_general(sk_ref[0], q_scr[2 * hd], dn, preferred_element_type=F32)
        s2_all = lax.dot_general(sk_ref[1], q_scr[2 * hd + 1], dn, preferred_element_type=F32)
        for lc in range(PEER_TR // LANES):
            cols = slice(lc * LANES, (lc + 1) * LANES)
            s1 = s1_all[:, cols]
            s2 = s2_all[:, cols]
            cur = s1
            for r in range(PEER_TOPK):
                m = jnp.max(cur, axis=0, keepdims=True)
                a_scr[r:r + 1, :] = m
                cur = jnp.where(cur >= m, NEG_INF, cur)
            cur = s2
            rank = jnp.full_like(s2, RANK_REST)
            for r in range(PEER_TOPK):
                m = jnp.max(cur, axis=0, keepdims=True)
                b_scr[r:r + 1, :] = m
                hit = cur >= m
                rank = jnp.where(hit, float(r), rank)
                cur = jnp.where(hit, NEG_INF, cur)
            a = a_scr[...]
            b = b_scr[...]
            blocks = [a[0:1, :] + b]
            for r in range(1, 8):
                blocks.append(a[r:r + 1, :] + b[0:8, :])
            blocks.append(a[8:16, :] + b[0:1, :])
            cur = jnp.concatenate(blocks, axis=0)
            top = a[0:1, :] + b[0:1, :]
            zsum = jnp.zeros_like(top)
            m = top
            for r in range(PEER_TOPK):
                m = jnp.max(cur, axis=0, keepdims=True)
                zsum = zsum + jnp.exp(m - top)
                cur = jnp.where(cur >= m, NEG_INF, cur)
            tau = m
            cnt = jnp.zeros_like(s1)
            for c in range(PEER_TOPK):
                cnt = jnp.where(s1 + b[c:c + 1, :] >= tau, float(c + 1), cnt)
            cnt_ref[hd, :, cols] = cnt
            e1_ref[hd, :, cols] = jnp.exp(s1 - a[0:1, :]) * (1.0 / zsum)
            rank_ref[hd, :, cols] = pltpu.bitcast(rank.astype(BF16), jnp.uint32)
            e2_ref[hd, :, cols] = pltpu.bitcast(jnp.exp(s2 - b[0:1, :]).astype(BF16), jnp.uint32)
        return carry

    lax.fori_loop(0, PEER_HEADS, head_body, 0)


def peer_route(x2, g, w_pq, sub_keys):
    n, d = x2.shape
    tr = PEER_TR
    qd = w_pq.shape[1]
    f32_shape = jax.ShapeDtypeStruct((PEER_HEADS, PEER_KEYS, n), F32)
    u32_shape = jax.ShapeDtypeStruct((PEER_HEADS, PEER_KEYS // 2, n), jnp.uint32)
    f32_spec = pl.BlockSpec((PEER_HEADS, PEER_KEYS, tr), lambda i: (0, 0, i))
    u32_spec = pl.BlockSpec((PEER_HEADS, PEER_KEYS // 2, tr), lambda i: (0, 0, i))
    return pl.pallas_call(
        _peer_route_kernel,
        grid=(n // tr,),
        in_specs=[
            pl.BlockSpec((tr, d), lambda i: (i, 0)),
            pl.BlockSpec((1, d), lambda i: (0, 0)),
            pl.BlockSpec((d, qd), lambda i: (0, 0)),
            pl.BlockSpec((2, PEER_KEYS, PEER_KEYS), lambda i: (0, 0, 0)),
        ],
        out_specs=[pl.BlockSpec((tr, d), lambda i: (i, 0)),
                   f32_spec, f32_spec, u32_spec, u32_spec],
        out_shape=[jax.ShapeDtypeStruct((n, d), BF16),
                   f32_shape, f32_shape, u32_shape, u32_shape],
        scratch_shapes=[
            pltpu.VMEM((2 * PEER_HEADS, tr, PEER_KEYS), BF16),
            pltpu.VMEM((PEER_TOPK, LANES), F32),
            pltpu.VMEM((PEER_TOPK, LANES), F32),
        ],
        compiler_params=_cparams(1),
        name="peer_route",
    )(x2, g, w_pq, sub_keys)


def _pack_kernel(x_ref, o_ref):
    o_ref[...] = pltpu.bitcast(x_ref[...].astype(BF16), jnp.uint32)


def pack_bf16(x, tm):
    r, c = x.shape
    return pl.pallas_call(
        _pack_kernel,
        grid=(r // tm,),
        in_specs=[pl.BlockSpec((tm, c), lambda i: (i, 0))],
        out_specs=pl.BlockSpec((tm // 2, c), lambda i: (i, 0)),
        out_shape=jax.ShapeDtypeStruct((r // 2, c), jnp.uint32),
        compiler_params=_cparams(1),
        name="pack_bf16",
    )(x)


PEER_TN = 512
PEER_EB = 1024
PEER_TC = 256


def _gelu_tanh(a):
    inner = math.sqrt(2.0 / math.pi) * (a + 0.044715 * (a * a * a))
    return 0.5 * a * (1.0 + jnp.tanh(inner))


BF16_ROWS = 16


def _peer_dense_kernel(h_ref, x_ref, u_ref, vt_ref, cnt_ref, e1_ref, rank_ref, e2_ref, gf_ref,
                       o_ref, at_scr, g_scr, acc_scr):
    e = pl.program_id(1)
    last = pl.num_programs(1) - 1
    tn = h_ref.shape[0]
    reps = PEER_KEYS // BF16_ROWS

    def row_bf16(ref, hd, ii, cols):
        r16 = jnp.broadcast_to(ref[hd, ii:ii + 1, cols], (BF16_ROWS, LANES)).astype(BF16)
        return jnp.concatenate([r16] * reps, axis=0)

    def scores(c):
        toks = slice(c * PEER_TC, (c + 1) * PEER_TC)
        at_scr[:, toks] = lax.dot_general(pltpu.bitcast(u_ref[...], BF16), h_ref[toks, :],
                                          (((1,), (1,)), ((), ())),
                                          preferred_element_type=F32)

    def gated(c):
        for ii in range(PEER_EB // PEER_KEYS):
            rows = slice(ii * PEER_KEYS, (ii + 1) * PEER_KEYS)
            for lc in range(c * PEER_TC // LANES, (c + 1) * PEER_TC // LANES):
                cols = slice(lc * LANES, (lc + 1) * LANES)
                wgt = jnp.zeros((PEER_KEYS, LANES), BF16)
                for hd in range(PEER_HEADS):
                    rank = pltpu.bitcast(rank_ref[hd, :, cols], BF16)
                    e2 = pltpu.bitcast(e2_ref[hd, :, cols], BF16)
                    kept = jnp.where(rank < row_bf16(cnt_ref, hd, ii, cols), e2,
                                     jnp.zeros((), BF16))
                    wgt = wgt + kept * row_bf16(e1_ref, hd, ii, cols)
                g_scr[rows, cols] = _gelu_tanh(at_scr[rows, cols].astype(BF16)) * wgt

    def combine(c):
        toks = slice(c * PEER_TC, (c + 1) * PEER_TC)
        out_t = jnp.dot(pltpu.bitcast(vt_ref[...], BF16), g_scr[:, toks],
                        preferred_element_type=F32)
        acc_scr[:, toks] = jnp.where(e == 0, out_t, acc_scr[:, toks] + out_t)

    nch = tn // PEER_TC
    scores(0)
    for c in range(nch):
        if c + 1 < nch:
            scores(c + 1)
        gated(c)
        combine(c)

    @pl.when(e == last)
    def _():
        x3 = x_ref[...] + acc_scr[...].T
        o_ref[...] = _rms(x3, gf_ref[...])


def peer_dense(h3, x2, u, vt, cnt, e1, rank, e2, g_final):
    n, d = x2.shape
    ne = 2 * u.shape[0]
    tn, eb = PEER_TN, PEER_EB
    ib = eb // PEER_KEYS
    tok_spec = pl.BlockSpec((PEER_HEADS, PEER_KEYS // 2, tn), lambda i, e: (0, 0, i))
    blk_spec = pl.BlockSpec((PEER_HEADS, ib, tn), lambda i, e: (0, e, i))
    return pl.pallas_call(
        _peer_dense_kernel,
        grid=(n // tn, ne // eb),
        in_specs=[
            pl.BlockSpec((tn, d), lambda i, e: (i, 0)),
            pl.BlockSpec((tn, d), lambda i, e: (i, 0)),
            pl.BlockSpec((eb // 2, d), lambda i, e: (e, 0)),
            pl.BlockSpec((d // 2, eb), lambda i, e: (0, e)),
            blk_spec, blk_spec, tok_spec, tok_spec,
            pl.BlockSpec((1, d), lambda i, e: (0, 0)),
        ],
        out_specs=pl.BlockSpec((tn, d), lambda i, e: (i, 0)),
        out_shape=jax.ShapeDtypeStruct((n, d), F32),
        scratch_shapes=[
            pltpu.VMEM((eb, tn), F32),
            pltpu.VMEM((eb, tn), BF16),
            pltpu.VMEM((d, tn), F32),
        ],
        compiler_params=_cparams(2),
        name="peer_dense",
    )(h3, x2, u, vt, cnt, e1, rank, e2, g_final)


def kernel(x, mem, norm_mix_g, w_in, b_gate, conv_dw_w, conv_dw_b, conv_ln_g, conv_ln_b,
           w_conv_out, w_attn_out, w_out, norm_xattn_g, norm_mem_g, w_xq, w_xkv, w_xo,
           norm_ffn_g, w_peer_q, peer_sub_keys, peer_u, peer_v, norm_final_g):
    batch, seq, d = x.shape
    mem_len = mem.shape[1]
    depth = w_in.shape[0]
    c = conv_dw_w.shape[2]
    d_att = w_attn_out.shape[1]
    n = batch * seq
    xf = x.reshape(n, d)
    memf = mem.reshape(batch * mem_len, d)
    row = lambda v: v.reshape(1, -1).astype(F32)
    q_col = 2 * c // SB_GW
    k_col = q_col + d_att // SB_GW
    v_col = k_col + d_att // SB_GW
    gate_col = (2 * c + 3 * d_att) // d

    for l in range(depth):
        proj = norm_matmul(xf, row(norm_mix_g[l]), w_in[l].astype(BF16), 1024, 1024)
        y = dwconv(proj, conv_dw_w[l], row(conv_dw_b[l]), batch, seq, 1024)
        gc = conv_out(y, proj, row(conv_ln_g[l]), row(conv_ln_b[l]), row(b_gate[l, :d]),
                      w_conv_out[l].astype(BF16), gate_col, 512)
        attn = stick_breaking(proj, batch, seq, q_col, k_col, v_col)
        x1 = merge(attn, gc, proj, xf, row(b_gate[l, d:]), w_attn_out[l].astype(BF16),
                   w_out[l].astype(BF16), gate_col + 1, 512)
        kv = norm_matmul(memf, row(norm_mem_g[l]), w_xkv[l].astype(BF16), 256, 1024)
        x2 = cross_attention(x1, row(norm_xattn_g[l]), kv, w_xq[l].astype(BF16),
                             w_xo[l].astype(BF16), seq, mem_len, 512)
        h3, cnt, e1, rank, e2 = peer_route(x2, row(norm_ffn_g[l]), w_peer_q[l].astype(BF16),
                                           peer_sub_keys[l].astype(BF16))
        last = l == depth - 1
        gf = row(norm_final_g) if last else None
        assert last, "the final RMSNorm is fused into the last layer's PEER stage"
        xf = peer_dense(h3, x2, pack_bf16(peer_u[l], 1024), pack_bf16(peer_v[l].T, 128),
                        cnt, e1, rank, e2, gf)
    return xf.reshape(batch, seq, d)
```

```python
import functools
import math

import jax
import jax.numpy as jnp
from jax import lax
from jax.experimental import pallas as pl
from jax.experimental.pallas import tpu as pltpu

F32 = jnp.float32
BF16 = jnp.bfloat16

D_MODEL = 1024
CONV_WIDTH = 31
SB_HEADS = 16
SB_HEAD_DIM = 64
XA_HEADS = 4
XA_HEAD_DIM = 256
PEER_HEADS = 8
PEER_KEYS = 128
PEER_TOPK = 16
EPS = 1e-6

LANES = 128
VMEM_LIMIT = 56 * 1024 * 1024

NEG_INF = float("-inf")


def _cparams(n_axes):
    return pltpu.CompilerParams(
        dimension_semantics=("arbitrary",) * n_axes, vmem_limit_bytes=VMEM_LIMIT)


def _rms(x, g):
    ms = jnp.mean(x * x, axis=-1, keepdims=True)
    return x * lax.rsqrt(ms + EPS) * g


def _norm_mm_kernel(x_ref, g_ref, w_ref, o_ref):
    h = _rms(x_ref[...], g_ref[...]).astype(BF16)
    o_ref[...] = jnp.dot(h, w_ref[...], preferred_element_type=F32).astype(o_ref.dtype)


def norm_matmul(x, g, w, tm, tn):
    n, d = x.shape
    m = w.shape[1]
    return pl.pallas_call(
        _norm_mm_kernel,
        grid=(m // tn, n // tm),
        in_specs=[
            pl.BlockSpec((tm, d), lambda j, i: (i, 0)),
            pl.BlockSpec((1, d), lambda j, i: (0, 0)),
            pl.BlockSpec((d, tn), lambda j, i: (0, j)),
        ],
        out_specs=pl.BlockSpec((tm, tn), lambda j, i: (i, j)),
        out_shape=jax.ShapeDtypeStruct((n, m), BF16),
        compiler_params=_cparams(2),
        name="norm_matmul",
    )(x, g, w)


CONV_HALO = 32
CONV_ROWS = 64
CONV_LANES = 128


def _dwconv_kernel(a_ref, gt_ref, ah_ref, gh_ref, w_ref, b_ref, y_ref, u_scr):
    ts = a_ref.shape[0]
    first = pl.program_id(1) == 0
    u_scr[CONV_HALO:CONV_HALO + ts, :] = (
        a_ref[...].astype(F32) * jax.nn.sigmoid(gt_ref[...].astype(F32)))
    uh = ah_ref[...].astype(F32) * jax.nn.sigmoid(gh_ref[...].astype(F32))
    u_scr[0:CONV_HALO, :] = jnp.where(first, 0.0, uh)
    off = CONV_HALO - (CONV_WIDTH - 1)
    for lt in range(CONV_LANES // LANES):
        cols = slice(lt * LANES, (lt + 1) * LANES)
        for r in range(ts // CONV_ROWS):
            r0 = r * CONV_ROWS
            acc = jnp.zeros((CONV_ROWS, LANES), F32) + b_ref[:, cols]
            for k in range(CONV_WIDTH):
                acc = acc + w_ref[k:k + 1, cols] * u_scr[r0 + off + k:r0 + off + k + CONV_ROWS, cols]
            y_ref[r0:r0 + CONV_ROWS, cols] = acc


def dwconv(proj, dw_w, dw_b, batch, seq, ts):
    n = proj.shape[0]
    c = dw_w.shape[1]
    nct = c // CONV_LANES
    spt = seq // ts
    hb = ts // CONV_HALO

    def halo_idx(col0):
        def f(b, i, cc):
            return (jnp.maximum((b * spt + i) * hb - 1, 0), col0 + cc)
        return f

    return pl.pallas_call(
        _dwconv_kernel,
        grid=(batch, spt, nct),
        in_specs=[
            pl.BlockSpec((ts, CONV_LANES), lambda b, i, cc: (b * spt + i, cc)),
            pl.BlockSpec((ts, CONV_LANES), lambda b, i, cc: (b * spt + i, nct + cc)),
            pl.BlockSpec((CONV_HALO, CONV_LANES), halo_idx(0)),
            pl.BlockSpec((CONV_HALO, CONV_LANES), halo_idx(nct)),
            pl.BlockSpec((CONV_WIDTH, CONV_LANES), lambda b, i, cc: (0, cc)),
            pl.BlockSpec((1, CONV_LANES), lambda b, i, cc: (0, cc)),
        ],
        out_specs=pl.BlockSpec((ts, CONV_LANES), lambda b, i, cc: (b * spt + i, cc)),
        out_shape=jax.ShapeDtypeStruct((n, c), F32),
        scratch_shapes=[pltpu.VMEM((CONV_HALO + ts, CONV_LANES), F32)],
        compiler_params=_cparams(3),
        name="dwconv",
    )(proj, proj, proj, proj, dw_w, dw_b)


def _conv_out_kernel(y_ref, gl_ref, lng_ref, lnb_ref, bg_ref, w_ref, o_ref):
    y = y_ref[...]
    mu = jnp.mean(y, axis=-1, keepdims=True)
    yc = y - mu
    var = jnp.mean(yc * yc, axis=-1, keepdims=True)
    yn = yc * lax.rsqrt(var + EPS) * lng_ref[...] + lnb_ref[...]
    act = (yn * jax.nn.sigmoid(yn)).astype(BF16)
    co = jnp.dot(act, w_ref[...], preferred_element_type=F32)
    gate = jax.nn.sigmoid(gl_ref[...].astype(F32) + bg_ref[...])
    o_ref[...] = (gate * co).astype(o_ref.dtype)


def conv_out(y, proj, ln_g, ln_b, bg_conv, w, gate_col_block, tm):
    n, c = y.shape
    d = w.shape[1]
    return pl.pallas_call(
        _conv_out_kernel,
        grid=(n // tm,),
        in_specs=[
            pl.BlockSpec((tm, c), lambda i: (i, 0)),
            pl.BlockSpec((tm, d), lambda i: (i, gate_col_block)),
            pl.BlockSpec((1, c), lambda i: (0, 0)),
            pl.BlockSpec((1, c), lambda i: (0, 0)),
            pl.BlockSpec((1, d), lambda i: (0, 0)),
            pl.BlockSpec((c, d), lambda i: (0, 0)),
        ],
        out_specs=pl.BlockSpec((tm, d), lambda i: (i, 0)),
        out_shape=jax.ShapeDtypeStruct((n, d), BF16),
        compiler_params=_cparams(1),
        name="conv_out",
    )(y, proj, ln_g, ln_b, bg_conv, w)


SB_TQ = 256
SB_TK = 256
SB_CHUNK = 32
SB_GROUP = 4
SB_GW = SB_GROUP * SB_HEAD_DIM


def _softplus(z):
    return jnp.maximum(z, 0.0) + jnp.log(1.0 + jnp.exp2(jnp.abs(z) * (-1.0 / math.log(2.0))))


def _sb_kernel(q_ref, k_ref, v_ref, tri_ref, o_ref, z_scr, sp_scr, w_scr, acc_scr, run_scr):
    qi = pl.program_id(2)
    tq, tk = SB_TQ, SB_TK
    heads = range(SB_GROUP)
    lane = lax.broadcasted_iota(jnp.int32, (1, SB_GW), 1)
    in_head = [(lane >= hd * SB_HEAD_DIM) & (lane < (hd + 1) * SB_HEAD_DIM) for hd in heads]
    qg = q_ref[...]
    scale = jnp.asarray(1.0 / math.sqrt(SB_HEAD_DIM), qg.dtype)
    qh = [jnp.where(in_head[hd], qg, jnp.zeros_like(qg)) * scale for hd in heads]
    chunks = [slice(c * SB_CHUNK, (c + 1) * SB_CHUNK) for c in range(tq // SB_CHUNK)]

    def causal(c):
        qpos = c * SB_CHUNK + lax.broadcasted_iota(jnp.int32, (SB_CHUNK, tk), 0)
        kpos = lax.broadcasted_iota(jnp.int32, (SB_CHUNK, tk), 1)
        return kpos < qpos

    def region(js, diags):
        slots = range(len(js))
        kbs = [k_ref[pl.ds(pl.multiple_of(j * tk, tk), tk), :] for j in js]
        vbs = [v_ref[pl.ds(pl.multiple_of(j * tk, tk), tk), :] for j in js]
        for s in slots:
            for hd in heads:
                z_scr[hd, s] = lax.dot_general(qh[hd], kbs[s], (((1,), (1,)), ((), ())),
                                               preferred_element_type=F32)
        for s in slots:
            for hd in heads:
                for c, rows in enumerate(chunks):
                    sp = _softplus(z_scr[hd, s, rows, :])
                    if diags[s]:
                        sp = jnp.where(causal(c), sp, 0.0)
                    sp_scr[hd, s, rows, :] = sp.astype(BF16)
        for s in slots:
            for hd in heads:
                inc = jnp.dot(sp_scr[hd, s], tri_ref[...], preferred_element_type=F32)
                for c, rows in enumerate(chunks):
                    logw = z_scr[hd, s, rows, :] + inc[rows, :]
                    if diags[s]:
                        w = jnp.where(causal(c), jnp.exp(logw), 0.0)
                        run_scr[hd, rows, :] = inc[rows, 0:1]
                    else:
                        run = run_scr[hd, rows, :]
                        w = jnp.exp(logw + run)
                        run_scr[hd, rows, :] = run + inc[rows, 0:1]
                    w_scr[hd, s, rows, :] = w.astype(BF16)
        for hd in heads:
            for s in slots:
                pv = jnp.dot(w_scr[hd, s], vbs[s], preferred_element_type=F32)
                if diags[s]:
                    acc_scr[hd] = pv
                else:
                    acc_scr[hd] += pv

    odd = qi % 2

    @pl.when(odd == 0)
    def _():
        region([qi], [True])

    @pl.when(odd == 1)
    def _():
        region([qi, qi - 1], [True, False])

    def body(p, carry):
        j = qi - odd - 1 - 2 * p
        region([j, j - 1], [False, False])
        return carry

    lax.fori_loop(0, (qi - odd) // 2, body, 0)
    out = acc_scr[0]
    for hd in range(1, SB_GROUP):
        out = jnp.where(in_head[hd], acc_scr[hd], out)
    o_ref[...] = out.astype(o_ref.dtype)


def _neg_tri(tk):
    j = lax.broadcasted_iota(jnp.int32, (tk, tk), 0)
    s = lax.broadcasted_iota(jnp.int32, (tk, tk), 1)
    return jnp.where(j >= s, -1.0, 0.0).astype(BF16)


def stick_breaking(proj, batch, seq, q_col, k_col, v_col):
    n = proj.shape[0]
    nq = seq // SB_TQ
    ngroup = SB_HEADS // SB_GROUP
    tri = _neg_tri(SB_TK)
    return pl.pallas_call(
        _sb_kernel,
        grid=(batch, ngroup, nq),
        in_specs=[
            pl.BlockSpec((SB_TQ, SB_GW), lambda b, g, qi: (b * nq + qi, q_col + g)),
            pl.BlockSpec((seq, SB_GW), lambda b, g, qi: (b, k_col + g)),
            pl.BlockSpec((seq, SB_GW), lambda b, g, qi: (b, v_col + g)),
            pl.BlockSpec((SB_TK, SB_TK), lambda b, g, qi: (0, 0)),
        ],
        out_specs=pl.BlockSpec((SB_TQ, SB_GW), lambda b, g, qi: (b * nq + qi, g)),
        out_shape=jax.ShapeDtypeStruct((n, SB_HEADS * SB_HEAD_DIM), BF16),
        scratch_shapes=[
            pltpu.VMEM((SB_GROUP, 2, SB_TQ, SB_TK), F32),
            pltpu.VMEM((SB_GROUP, 2, SB_TQ, SB_TK), BF16),
            pltpu.VMEM((SB_GROUP, 2, SB_TQ, SB_TK), BF16),
            pltpu.VMEM((SB_GROUP, SB_TQ, SB_GW), F32),
            pltpu.VMEM((SB_GROUP, SB_TQ, 1), F32),
        ],
        compiler_params=_cparams(3),
        name="stick_breaking",
    )(proj, proj, proj, tri)


def _merge_kernel(attn_ref, gc_ref, gl_ref, x_ref, bg_ref, wa_ref, wo_ref, o_ref):
    ao = jnp.dot(attn_ref[...], wa_ref[...], preferred_element_type=F32)
    gate = jax.nn.sigmoid(gl_ref[...].astype(F32) + bg_ref[...])
    mixed = (gc_ref[...].astype(F32) + gate * ao).astype(BF16)
    o_ref[...] = x_ref[...] + jnp.dot(mixed, wo_ref[...], preferred_element_type=F32)


def merge(attn, gc, proj, x, bg_attn, w_attn_out, w_out, gate_col_block, tm):
    n, d = x.shape
    da = attn.shape[1]
    return pl.pallas_call(
        _merge_kernel,
        grid=(n // tm,),
        in_specs=[
            pl.BlockSpec((tm, da), lambda i: (i, 0)),
            pl.BlockSpec((tm, d), lambda i: (i, 0)),
            pl.BlockSpec((tm, d), lambda i: (i, gate_col_block)),
            pl.BlockSpec((tm, d), lambda i: (i, 0)),
            pl.BlockSpec((1, d), lambda i: (0, 0)),
            pl.BlockSpec((da, d), lambda i: (0, 0)),
            pl.BlockSpec((d, d), lambda i: (0, 0)),
        ],
        out_specs=pl.BlockSpec((tm, d), lambda i: (i, 0)),
        out_shape=jax.ShapeDtypeStruct((n, d), F32),
        compiler_params=_cparams(1),
        name="merge",
    )(attn, gc, proj, x, bg_attn, w_attn_out, w_out)


def _xattn_kernel(x_ref, g_ref, kv_ref, wq_ref, wo_ref, o_ref):
    x = x_ref[...]
    h = _rms(x, g_ref[...]).astype(BF16)
    q = jnp.dot(h, wq_ref[...], preferred_element_type=F32)
    q = (q * (1.0 / math.sqrt(XA_HEAD_DIM))).astype(BF16)
    d = XA_HEADS * XA_HEAD_DIM
    outs = []
    for hd in range(XA_HEADS):
        cols = slice(hd * XA_HEAD_DIM, (hd + 1) * XA_HEAD_DIM)
        kh = kv_ref[:, cols]
        vh = kv_ref[:, d + hd * XA_HEAD_DIM:d + (hd + 1) * XA_HEAD_DIM]
        sc = lax.dot_general(q[:, cols], kh, (((1,), (1,)), ((), ())),
                             preferred_element_type=F32)
        p = jnp.exp(sc - jnp.max(sc, axis=-1, keepdims=True))
        p = (p / jnp.sum(p, axis=-1, keepdims=True)).astype(BF16)
        outs.append(jnp.dot(p, vh, preferred_element_type=F32).astype(BF16))
    o = jnp.concatenate(outs, axis=1)
    o_ref[...] = x + jnp.dot(o, wo_ref[...], preferred_element_type=F32)


def cross_attention(x1, g, kv, w_xq, w_xo, seq, mem_len, tm):
    n, d = x1.shape
    tiles_per_batch = seq // tm
    return pl.pallas_call(
        _xattn_kernel,
        grid=(n // tm,),
        in_specs=[
            pl.BlockSpec((tm, d), lambda i: (i, 0)),
            pl.BlockSpec((1, d), lambda i: (0, 0)),
            pl.BlockSpec((mem_len, 2 * d), lambda i: (i // tiles_per_batch, 0)),
            pl.BlockSpec((d, d), lambda i: (0, 0)),
            pl.BlockSpec((d, d), lambda i: (0, 0)),
        ],
        out_specs=pl.BlockSpec((tm, d), lambda i: (i, 0)),
        out_shape=jax.ShapeDtypeStruct((n, d), F32),
        compiler_params=_cparams(1),
        name="cross_attention",
    )(x1, g, kv, w_xq, w_xo)


PEER_TR = 256
RANK_REST = 127.0


def _peer_route_kernel(x_ref, g_ref, wq_ref, sk_ref, h_ref, cnt_ref, e1_ref, rank_ref, e2_ref,
                       q_scr, a_scr, b_scr):
    h3 = _rms(x_ref[...], g_ref[...]).astype(BF16)
    h_ref[...] = h3
    q = jnp.dot(h3, wq_ref[...], preferred_element_type=F32).astype(BF16)
    for c in range(2 * PEER_HEADS):
        q_scr[c] = q[:, c * PEER_KEYS:(c + 1) * PEER_KEYS]

    def head_body(hd, carry):
        dn = (((1,), (1,)), ((), ()))
        s1_all = lax.dot_general(sk_ref[0], q_scr[2 * hd], dn, preferred_element_type=F32)
        s2_all = lax.dot_general(sk_ref[1], q_scr[2 * hd + 1], dn, preferred_element_type=F32)
        for lc in range(PEER_TR // LANES):
            cols = slice(lc * LANES, (lc + 1) * LANES)
            s1 = s1_all[:, cols]
            s2 = s2_all[:, cols]
            cur = s1
            for r in range(PEER_TOPK):
                m = jnp.max(cur, axis=0, keepdims=True)
                a_scr[r:r + 1, :] = m
                cur = jnp.where(cur >= m, NEG_INF, cur)
            cur = s2
            rank = jnp.full_like(s2, RANK_REST)
            for r in range(PEER_TOPK):
                m = jnp.max(cur, axis=0, keepdims=True)
                b_scr[r:r + 1, :] = m
                hit = cur >= m
                rank = jnp.where(hit, float(r), rank)
                cur = jnp.where(hit, NEG_INF, cur)
            a = a_scr[...]
            b = b_scr[...]
            blocks = [a[0:1, :] + b]
            for r in range(1, 8):
                blocks.append(a[r:r + 1, :] + b[0:8, :])
            blocks.append(a[8:16, :] + b[0:1, :])
            cur = jnp.concatenate(blocks, axis=0)
            top = a[0:1, :] + b[0:1, :]
            zsum = jnp.zeros_like(top)
            m = top
            for r in range(PEER_TOPK):
                m = jnp.max(cur, axis=0, keepdims=True)
                zsum = zsum + jnp.exp(m - top)
                cur = jnp.where(cur >= m, NEG_INF, cur)
            tau = m
            cnt = jnp.zeros_like(s1)
            for c in range(PEER_TOPK):
                cnt = jnp.where(s1 + b[c:c + 1, :] >= tau, float(c + 1), cnt)
            cnt_ref[hd, :, cols] = cnt
            e1_ref[hd, :, cols] = jnp.exp(s1 - a[0:1, :]) * (1.0 / zsum)
            rank_ref[hd, :, cols] = pltpu.bitcast(rank.astype(BF16), jnp.uint32)
            e2_ref[hd, :, cols] = pltpu.bitcast(jnp.exp(s2 - b[0:1, :]).astype(BF16), jnp.uint32)
        return carry

    lax.fori_loop(0, PEER_HEADS, head_body, 0)


def peer_route(x2, g, w_pq, sub_keys):
    n, d = x2.shape
    tr = PEER_TR
    qd = w_pq.shape[1]
    f32_shape = jax.ShapeDtypeStruct((PEER_HEADS, PEER_KEYS, n), F32)
    u32_shape = jax.ShapeDtypeStruct((PEER_HEADS, PEER_KEYS // 2, n), jnp.uint32)
    f32_spec = pl.BlockSpec((PEER_HEADS, PEER_KEYS, tr), lambda i: (0, 0, i))
    u32_spec = pl.BlockSpec((PEER_HEADS, PEER_KEYS // 2, tr), lambda i: (0, 0, i))
    return pl.pallas_call(
        _peer_route_kernel,
        grid=(n // tr,),
        in_specs=[
            pl.BlockSpec((tr, d), lambda i: (i, 0)),
            pl.BlockSpec((1, d), lambda i: (0, 0)),
            pl.BlockSpec((d, qd), lambda i: (0, 0)),
            pl.BlockSpec((2, PEER_KEYS, PEER_KEYS), lambda i: (0, 0, 0)),
        ],
        out_specs=[pl.BlockSpec((tr, d), lambda i: (i, 0)),
                   f32_spec, f32_spec, u32_spec, u32_spec],
        out_shape=[jax.ShapeDtypeStruct((n, d), BF16),
                   f32_shape, f32_shape, u32_shape, u32_shape],
        scratch_shapes=[
            pltpu.VMEM((2 * PEER_HEADS, tr, PEER_KEYS), BF16),
            pltpu.VMEM((PEER_TOPK, LANES), F32),
            pltpu.VMEM((PEER_TOPK, LANES), F32),
        ],
        compiler_params=_cparams(1),
        name="peer_route",
    )(x2, g, w_pq, sub_keys)


PEER_TN = 512
PEER_EB = 1024
PEER_TC = 256


def _gelu_tanh(a):
    inner = math.sqrt(2.0 / math.pi) * (a + 0.044715 * (a * a * a))
    return 0.5 * a * (1.0 + jnp.tanh(inner))


BF16_ROWS = 16


def _peer_dense_kernel(h_ref, x_ref, u_ref, vt_ref, cnt_ref, e1_ref, rank_ref, e2_ref, gf_ref,
                       o_ref, at_scr, g_scr, acc_scr):
    e = pl.program_id(1)
    last = pl.num_programs(1) - 1
    tn = h_ref.shape[0]
    reps = PEER_KEYS // BF16_ROWS

    def row_bf16(ref, hd, ii, cols):
        r16 = jnp.broadcast_to(ref[hd, ii:ii + 1, cols], (BF16_ROWS, LANES)).astype(BF16)
        return jnp.concatenate([r16] * reps, axis=0)

    def scores(c):
        toks = slice(c * PEER_TC, (c + 1) * PEER_TC)
        at_scr[:, toks] = lax.dot_general(u_ref[...], h_ref[toks, :], (((1,), (1,)), ((), ())),
                                          preferred_element_type=F32)

    def gated(c):
        for ii in range(PEER_EB // PEER_KEYS):
            rows = slice(ii * PEER_KEYS, (ii + 1) * PEER_KEYS)
            for lc in range(c * PEER_TC // LANES, (c + 1) * PEER_TC // LANES):
                cols = slice(lc * LANES, (lc + 1) * LANES)
                wgt = jnp.zeros((PEER_KEYS, LANES), BF16)
                for hd in range(PEER_HEADS):
                    rank = pltpu.bitcast(rank_ref[hd, :, cols], BF16)
                    e2 = pltpu.bitcast(e2_ref[hd, :, cols], BF16)
                    kept = jnp.where(rank < row_bf16(cnt_ref, hd, ii, cols), e2,
                                     jnp.zeros((), BF16))
                    wgt = wgt + kept * row_bf16(e1_ref, hd, ii, cols)
                g_scr[rows, cols] = _gelu_tanh(at_scr[rows, cols].astype(BF16)) * wgt

    def combine(c):
        toks = slice(c * PEER_TC, (c + 1) * PEER_TC)
        out_t = jnp.dot(vt_ref[...], g_scr[:, toks], preferred_element_type=F32)
        acc_scr[:, toks] = jnp.where(e == 0, out_t, acc_scr[:, toks] + out_t)

    nch = tn // PEER_TC
    scores(0)
    for c in range(nch):
        if c + 1 < nch:
            scores(c + 1)
        gated(c)
        combine(c)

    @pl.when(e == last)
    def _():
        x3 = x_ref[...] + acc_scr[...].T
        o_ref[...] = _rms(x3, gf_ref[...])


def peer_dense(h3, x2, u, vt, cnt, e1, rank, e2, g_final):
    n, d = x2.shape
    ne = u.shape[0]
    tn, eb = PEER_TN, PEER_EB
    ib = eb // PEER_KEYS
    tok_spec = pl.BlockSpec((PEER_HEADS, PEER_KEYS // 2, tn), lambda i, e: (0, 0, i))
    blk_spec = pl.BlockSpec((PEER_HEADS, ib, tn), lambda i, e: (0, e, i))
    return pl.pallas_call(
        _peer_dense_kernel,
        grid=(n // tn, ne // eb),
        in_specs=[
            pl.BlockSpec((tn, d), lambda i, e: (i, 0)),
            pl.BlockSpec((tn, d), lambda i, e: (i, 0)),
            pl.BlockSpec((eb, d), lambda i, e: (e, 0)),
            pl.BlockSpec((d, eb), lambda i, e: (0, e)),
            blk_spec, blk_spec, tok_spec, tok_spec,
            pl.BlockSpec((1, d), lambda i, e: (0, 0)),
        ],
        out_specs=pl.BlockSpec((tn, d), lambda i, e: (i, 0)),
        out_shape=jax.ShapeDtypeStruct((n, d), F32),
        scratch_shapes=[
            pltpu.VMEM((eb, tn), F32),
            pltpu.VMEM((eb, tn), BF16),
            pltpu.VMEM((d, tn), F32),
        ],
        compiler_params=_cparams(2),
        name="peer_dense",
    )(h3, x2, u, vt, cnt, e1, rank, e2, g_final)


def kernel(x, mem, norm_mix_g, w_in, b_gate, conv_dw_w, conv_dw_b, conv_ln_g, conv_ln_b,
           w_conv_out, w_attn_out, w_out, norm_xattn_g, norm_mem_g, w_xq, w_xkv, w_xo,
           norm_ffn_g, w_peer_q, peer_sub_keys, peer_u, peer_v, norm_final_g):
    batch, seq, d = x.shape
    mem_len = mem.shape[1]
    depth = w_in.shape[0]
    c = conv_dw_w.shape[2]
    d_att = w_attn_out.shape[1]
    n = batch * seq
    xf = x.reshape(n, d)
    memf = mem.reshape(batch * mem_len, d)
    row = lambda v: v.reshape(1, -1).astype(F32)
    q_col = 2 * c // SB_GW
    k_col = q_col + d_att // SB_GW
    v_col = k_col + d_att // SB_GW
    gate_col = (2 * c + 3 * d_att) // d

    for l in range(depth):
        proj = norm_matmul(xf, row(norm_mix_g[l]), w_in[l].astype(BF16), 512, w_in.shape[2] // 2)
        y = dwconv(proj, conv_dw_w[l], row(conv_dw_b[l]), batch, seq, 1024)
        gc = conv_out(y, proj, row(conv_ln_g[l]), row(conv_ln_b[l]), row(b_gate[l, :d]),
                      w_conv_out[l].astype(BF16), gate_col, 512)
        attn = stick_breaking(proj, batch, seq, q_col, k_col, v_col)
        x1 = merge(attn, gc, proj, xf, row(b_gate[l, d:]), w_attn_out[l].astype(BF16),
                   w_out[l].astype(BF16), gate_col + 1, 512)
        kv = norm_matmul(memf, row(norm_mem_g[l]), w_xkv[l].astype(BF16), 256, 1024)
        x2 = cross_attention(x1, row(norm_xattn_g[l]), kv, w_xq[l].astype(BF16),
                             w_xo[l].astype(BF16), seq, mem_len, 512)
        h3, cnt, e1, rank, e2 = peer_route(x2, row(norm_ffn_g[l]), w_peer_q[l].astype(BF16),
                                           peer_sub_keys[l].astype(BF16))
        last = l == depth - 1
        gf = row(norm_final_g) if last else None
        assert last, "the final RMSNorm is fused into the last layer's PEER stage"
        xf = peer_dense(h3, x2, peer_u[l].astype(BF16), peer_v[l].astype(BF16).T,
                        cnt, e1, rank, e2, gf)
    return xf.reshape(batch, seq, d)
```

```python
import functools
import math

import jax
import jax.numpy as jnp
from jax import lax
from jax.experimental import pallas as pl
from jax.experimental.pallas import tpu as pltpu

F32 = jnp.float32
BF16 = jnp.bfloat16

D_MODEL = 1024
CONV_WIDTH = 31
SB_HEADS = 16
SB_HEAD_DIM = 64
XA_HEADS = 4
XA_HEAD_DIM = 256
PEER_HEADS = 8
PEER_KEYS = 128
PEER_TOPK = 16
EPS = 1e-6

LANES = 128
VMEM_LIMIT = 56 * 1024 * 1024

NEG_INF = float("-inf")


def _cparams(n_axes):
    return pltpu.CompilerParams(
        dimension_semantics=("arbitrary",) * n_axes, vmem_limit_bytes=VMEM_LIMIT)


def _rms(x, g):
    ms = jnp.mean(x * x, axis=-1, keepdims=True)
    return x * lax.rsqrt(ms + EPS) * g


def _norm_mm_kernel(x_ref, g_ref, w_ref, o_ref):
    h = _rms(x_ref[...], g_ref[...]).astype(BF16)
    o_ref[...] = jnp.dot(h, w_ref[...], preferred_element_type=F32).astype(o_ref.dtype)


def norm_matmul(x, g, w, tm, tn):
    n, d = x.shape
    m = w.shape[1]
    return pl.pallas_call(
        _norm_mm_kernel,
        grid=(m // tn, n // tm),
        in_specs=[
            pl.BlockSpec((tm, d), lambda j, i: (i, 0)),
            pl.BlockSpec((1, d), lambda j, i: (0, 0)),
            pl.BlockSpec((d, tn), lambda j, i: (0, j)),
        ],
        out_specs=pl.BlockSpec((tm, tn), lambda j, i: (i, j)),
        out_shape=jax.ShapeDtypeStruct((n, m), BF16),
        compiler_params=_cparams(2),
        name="norm_matmul",
    )(x, g, w)


CONV_HALO = 32
CONV_ROWS = 64
CONV_LANES = 128


def _dwconv_kernel(a_ref, gt_ref, ah_ref, gh_ref, w_ref, b_ref, y_ref, u_scr):
    ts = a_ref.shape[0]
    first = pl.program_id(1) == 0
    u_scr[CONV_HALO:CONV_HALO + ts, :] = (
        a_ref[...].astype(F32) * jax.nn.sigmoid(gt_ref[...].astype(F32)))
    uh = ah_ref[...].astype(F32) * jax.nn.sigmoid(gh_ref[...].astype(F32))
    u_scr[0:CONV_HALO, :] = jnp.where(first, 0.0, uh)
    off = CONV_HALO - (CONV_WIDTH - 1)
    for lt in range(CONV_LANES // LANES):
        cols = slice(lt * LANES, (lt + 1) * LANES)
        for r in range(ts // CONV_ROWS):
            r0 = r * CONV_ROWS
            acc = jnp.zeros((CONV_ROWS, LANES), F32) + b_ref[:, cols]
            for k in range(CONV_WIDTH):
                acc = acc + w_ref[k:k + 1, cols] * u_scr[r0 + off + k:r0 + off + k + CONV_ROWS, cols]
            y_ref[r0:r0 + CONV_ROWS, cols] = acc


def dwconv(proj, dw_w, dw_b, batch, seq, ts):
    n = proj.shape[0]
    c = dw_w.shape[1]
    nct = c // CONV_LANES
    spt = seq // ts
    hb = ts // CONV_HALO

    def halo_idx(col0):
        def f(b, i, cc):
            return (jnp.maximum((b * spt + i) * hb - 1, 0), col0 + cc)
        return f

    return pl.pallas_call(
        _dwconv_kernel,
        grid=(batch, spt, nct),
        in_specs=[
            pl.BlockSpec((ts, CONV_LANES), lambda b, i, cc: (b * spt + i, cc)),
            pl.BlockSpec((ts, CONV_LANES), lambda b, i, cc: (b * spt + i, nct + cc)),
            pl.BlockSpec((CONV_HALO, CONV_LANES), halo_idx(0)),
            pl.BlockSpec((CONV_HALO, CONV_LANES), halo_idx(nct)),
            pl.BlockSpec((CONV_WIDTH, CONV_LANES), lambda b, i, cc: (0, cc)),
            pl.BlockSpec((1, CONV_LANES), lambda b, i, cc: (0, cc)),
        ],
        out_specs=pl.BlockSpec((ts, CONV_LANES), lambda b, i, cc: (b * spt + i, cc)),
        out_shape=jax.ShapeDtypeStruct((n, c), F32),
        scratch_shapes=[pltpu.VMEM((CONV_HALO + ts, CONV_LANES), F32)],
        compiler_params=_cparams(3),
        name="dwconv",
    )(proj, proj, proj, proj, dw_w, dw_b)


def _conv_out_kernel(y_ref, gl_ref, lng_ref, lnb_ref, bg_ref, w_ref, o_ref):
    y = y_ref[...]
    mu = jnp.mean(y, axis=-1, keepdims=True)
    yc = y - mu
    var = jnp.mean(yc * yc, axis=-1, keepdims=True)
    yn = yc * lax.rsqrt(var + EPS) * lng_ref[...] + lnb_ref[...]
    act = (yn * jax.nn.sigmoid(yn)).astype(BF16)
    co = jnp.dot(act, w_ref[...], preferred_element_type=F32)
    gate = jax.nn.sigmoid(gl_ref[...].astype(F32) + bg_ref[...])
    o_ref[...] = (gate * co).astype(o_ref.dtype)


def conv_out(y, proj, ln_g, ln_b, bg_conv, w, gate_col_block, tm):
    n, c = y.shape
    d = w.shape[1]
    return pl.pallas_call(
        _conv_out_kernel,
        grid=(n // tm,),
        in_specs=[
            pl.BlockSpec((tm, c), lambda i: (i, 0)),
            pl.BlockSpec((tm, d), lambda i: (i, gate_col_block)),
            pl.BlockSpec((1, c), lambda i: (0, 0)),
            pl.BlockSpec((1, c), lambda i: (0, 0)),
            pl.BlockSpec((1, d), lambda i: (0, 0)),
            pl.BlockSpec((c, d), lambda i: (0, 0)),
        ],
        out_specs=pl.BlockSpec((tm, d), lambda i: (i, 0)),
        out_shape=jax.ShapeDtypeStruct((n, d), BF16),
        compiler_params=_cparams(1),
        name="conv_out",
    )(y, proj, ln_g, ln_b, bg_conv, w)


SB_TQ = 256
SB_TK = 256
SB_CHUNK = 32
SB_GROUP = 4
SB_GW = SB_GROUP * SB_HEAD_DIM


def _softplus(z):
    return jnp.maximum(z, 0.0) + jnp.log(1.0 + jnp.exp2(jnp.abs(z) * (-1.0 / math.log(2.0))))


def _sb_kernel(q_ref, k_ref, v_ref, tri_ref, o_ref, z_scr, sp_scr, w_scr, acc_scr, run_scr):
    qi = pl.program_id(2)
    tq, tk = SB_TQ, SB_TK
    heads = range(SB_GROUP)
    lane = lax.broadcasted_iota(jnp.int32, (1, SB_GW), 1)
    in_head = [(lane >= hd * SB_HEAD_DIM) & (lane < (hd + 1) * SB_HEAD_DIM) for hd in heads]
    qg = q_ref[...]
    scale = jnp.asarray(1.0 / math.sqrt(SB_HEAD_DIM), qg.dtype)
    qh = [jnp.where(in_head[hd], qg, jnp.zeros_like(qg)) * scale for hd in heads]
    chunks = [slice(c * SB_CHUNK, (c + 1) * SB_CHUNK) for c in range(tq // SB_CHUNK)]

    def causal(c):
        qpos = c * SB_CHUNK + lax.broadcasted_iota(jnp.int32, (SB_CHUNK, tk), 0)
        kpos = lax.broadcasted_iota(jnp.int32, (SB_CHUNK, tk), 1)
        return kpos < qpos

    def region(js, diags):
        slots = range(len(js))
        kbs = [k_ref[pl.ds(pl.multiple_of(j * tk, tk), tk), :] for j in js]
        vbs = [v_ref[pl.ds(pl.multiple_of(j * tk, tk), tk), :] for j in js]
        for s in slots:
            for hd in heads:
                z_scr[hd, s] = lax.dot_general(qh[hd], kbs[s], (((1,), (1,)), ((), ())),
                                               preferred_element_type=F32)
        for s in slots:
            for hd in heads:
                for c, rows in enumerate(chunks):
                    sp = _softplus(z_scr[hd, s, rows, :])
                    if diags[s]:
                        sp = jnp.where(causal(c), sp, 0.0)
                    sp_scr[hd, s, rows, :] = sp.astype(BF16)
        for s in slots:
            for hd in heads:
                inc = jnp.dot(sp_scr[hd, s], tri_ref[...], preferred_element_type=F32)
                for c, rows in enumerate(chunks):
                    logw = z_scr[hd, s, rows, :] + inc[rows, :]
                    if diags[s]:
                        w = jnp.where(causal(c), jnp.exp(logw), 0.0)
                        run_scr[hd, rows, :] = inc[rows, 0:1]
                    else:
                        run = run_scr[hd, rows, :]
                        w = jnp.exp(logw + run)
                        run_scr[hd, rows, :] = run + inc[rows, 0:1]
                    w_scr[hd, s, rows, :] = w.astype(BF16)
        for hd in heads:
            for s in slots:
                pv = jnp.dot(w_scr[hd, s], vbs[s], preferred_element_type=F32)
                if diags[s]:
                    acc_scr[hd] = pv
                else:
                    acc_scr[hd] += pv

    odd = qi % 2

    @pl.when(odd == 0)
    def _():
        region([qi], [True])

    @pl.when(odd == 1)
    def _():
        region([qi, qi - 1], [True, False])

    def body(p, carry):
        j = qi - odd - 1 - 2 * p
        region([j, j - 1], [False, False])
        return carry

    lax.fori_loop(0, (qi - odd) // 2, body, 0)
    out = acc_scr[0]
    for hd in range(1, SB_GROUP):
        out = jnp.where(in_head[hd], acc_scr[hd], out)
    o_ref[...] = out.astype(o_ref.dtype)


def _neg_tri(tk):
    j = lax.broadcasted_iota(jnp.int32, (tk, tk), 0)
    s = lax.broadcasted_iota(jnp.int32, (tk, tk), 1)
    return jnp.where(j >= s, -1.0, 0.0).astype(BF16)


def stick_breaking(proj, batch, seq, q_col, k_col, v_col):
    n = proj.shape[0]
    nq = seq // SB_TQ
    ngroup = SB_HEADS // SB_GROUP
    tri = _neg_tri(SB_TK)
    return pl.pallas_call(
        _sb_kernel,
        grid=(batch, ngroup, nq),
        in_specs=[
            pl.BlockSpec((SB_TQ, SB_GW), lambda b, g, qi: (b * nq + qi, q_col + g)),
            pl.BlockSpec((seq, SB_GW), lambda b, g, qi: (b, k_col + g)),
            pl.BlockSpec((seq, SB_GW), lambda b, g, qi: (b, v_col + g)),
            pl.BlockSpec((SB_TK, SB_TK), lambda b, g, qi: (0, 0)),
        ],
        out_specs=pl.BlockSpec((SB_TQ, SB_GW), lambda b, g, qi: (b * nq + qi, g)),
        out_shape=jax.ShapeDtypeStruct((n, SB_HEADS * SB_HEAD_DIM), BF16),
        scratch_shapes=[
            pltpu.VMEM((SB_GROUP, 2, SB_TQ, SB_TK), F32),
            pltpu.VMEM((SB_GROUP, 2, SB_TQ, SB_TK), BF16),
            pltpu.VMEM((SB_GROUP, 2, SB_TQ, SB_TK), BF16),
            pltpu.VMEM((SB_GROUP, SB_TQ, SB_GW), F32),
            pltpu.VMEM((SB_GROUP, SB_TQ, 1), F32),
        ],
        compiler_params=_cparams(3),
        name="stick_breaking",
    )(proj, proj, proj, tri)


def _merge_kernel(attn_ref, gc_ref, gl_ref, x_ref, bg_ref, wa_ref, wo_ref, o_ref):
    ao = jnp.dot(attn_ref[...], wa_ref[...], preferred_element_type=F32)
    gate = jax.nn.sigmoid(gl_ref[...].astype(F32) + bg_ref[...])
    mixed = (gc_ref[...].astype(F32) + gate * ao).astype(BF16)
    o_ref[...] = x_ref[...] + jnp.dot(mixed, wo_ref[...], preferred_element_type=F32)


def merge(attn, gc, proj, x, bg_attn, w_attn_out, w_out, gate_col_block, tm):
    n, d = x.shape
    da = attn.shape[1]
    return pl.pallas_call(
        _merge_kernel,
        grid=(n // tm,),
        in_specs=[
            pl.BlockSpec((tm, da), lambda i: (i, 0)),
            pl.BlockSpec((tm, d), lambda i: (i, 0)),
            pl.BlockSpec((tm, d), lambda i: (i, gate_col_block)),
            pl.BlockSpec((tm, d), lambda i: (i, 0)),
            pl.BlockSpec((1, d), lambda i: (0, 0)),
            pl.BlockSpec((da, d), lambda i: (0, 0)),
            pl.BlockSpec((d, d), lambda i: (0, 0)),
        ],
        out_specs=pl.BlockSpec((tm, d), lambda i: (i, 0)),
        out_shape=jax.ShapeDtypeStruct((n, d), F32),
        compiler_params=_cparams(1),
        name="merge",
    )(attn, gc, proj, x, bg_attn, w_attn_out, w_out)


def _xattn_kernel(x_ref, g_ref, kv_ref, wq_ref, wo_ref, o_ref):
    x = x_ref[...]
    h = _rms(x, g_ref[...]).astype(BF16)
    q = jnp.dot(h, wq_ref[...], preferred_element_type=F32)
    q = (q * (1.0 / math.sqrt(XA_HEAD_DIM))).astype(BF16)
    d = XA_HEADS * XA_HEAD_DIM
    outs = []
    for hd in range(XA_HEADS):
        cols = slice(hd * XA_HEAD_DIM, (hd + 1) * XA_HEAD_DIM)
        kh = kv_ref[:, cols]
        vh = kv_ref[:, d + hd * XA_HEAD_DIM:d + (hd + 1) * XA_HEAD_DIM]
        sc = lax.dot_general(q[:, cols], kh, (((1,), (1,)), ((), ())),
                             preferred_element_type=F32)
        p = jnp.exp(sc - jnp.max(sc, axis=-1, keepdims=True))
        p = (p / jnp.sum(p, axis=-1, keepdims=True)).astype(BF16)
        outs.append(jnp.dot(p, vh, preferred_element_type=F32).astype(BF16))
    o = jnp.concatenate(outs, axis=1)
    o_ref[...] = x + jnp.dot(o, wo_ref[...], preferred_element_type=F32)


def cross_attention(x1, g, kv, w_xq, w_xo, seq, mem_len, tm):
    n, d = x1.shape
    tiles_per_batch = seq // tm
    return pl.pallas_call(
        _xattn_kernel,
        grid=(n // tm,),
        in_specs=[
            pl.BlockSpec((tm, d), lambda i: (i, 0)),
            pl.BlockSpec((1, d), lambda i: (0, 0)),
            pl.BlockSpec((mem_len, 2 * d), lambda i: (i // tiles_per_batch, 0)),
            pl.BlockSpec((d, d), lambda i: (0, 0)),
            pl.BlockSpec((d, d), lambda i: (0, 0)),
        ],
        out_specs=pl.BlockSpec((tm, d), lambda i: (i, 0)),
        out_shape=jax.ShapeDtypeStruct((n, d), F32),
        compiler_params=_cparams(1),
        name="cross_attention",
    )(x1, g, kv, w_xq, w_xo)


PEER_TR = 256
RANK_REST = 127.0


def _peer_route_kernel(x_ref, g_ref, wq_ref, sk_ref, h_ref, cnt_ref, e1_ref, rank_ref, e2_ref,
                       q_scr, a_scr, b_scr):
    h3 = _rms(x_ref[...], g_ref[...]).astype(BF16)
    h_ref[...] = h3
    q = jnp.dot(h3, wq_ref[...], preferred_element_type=F32).astype(BF16)
    for c in range(2 * PEER_HEADS):
        q_scr[c] = q[:, c * PEER_KEYS:(c + 1) * PEER_KEYS]

    def head_body(hd, carry):
        dn = (((1,), (1,)), ((), ()))
        s1_all = lax.dot_general(sk_ref[0], q_scr[2 * hd], dn, preferred_element_type=F32)
        s2_all = lax.dot_general(sk_ref[1], q_scr[2 * hd + 1], dn, preferred_element_type=F32)
        for lc in range(PEER_TR // LANES):
            cols = slice(lc * LANES, (lc + 1) * LANES)
            s1 = s1_all[:, cols]
            s2 = s2_all[:, cols]
            cur = s1
            for r in range(PEER_TOPK):
                m = jnp.max(cur, axis=0, keepdims=True)
                a_scr[r:r + 1, :] = m
                cur = jnp.where(cur >= m, NEG_INF, cur)
            cur = s2
            rank = jnp.full_like(s2, RANK_REST)
            for r in range(PEER_TOPK):
                m = jnp.max(cur, axis=0, keepdims=True)
                b_scr[r:r + 1, :] = m
                hit = cur >= m
                rank = jnp.where(hit, float(r), rank)
                cur = jnp.where(hit, NEG_INF, cur)
            a = a_scr[...]
            b = b_scr[...]
            blocks = [a[0:1, :] + b]
            for r in range(1, 8):
                blocks.append(a[r:r + 1, :] + b[0:8, :])
            blocks.append(a[8:16, :] + b[0:1, :])
            cur = jnp.concatenate(blocks, axis=0)
            top = a[0:1, :] + b[0:1, :]
            zsum = jnp.zeros_like(top)
            m = top
            for r in range(PEER_TOPK):
                m = jnp.max(cur, axis=0, keepdims=True)
                zsum = zsum + jnp.exp(m - top)
                cur = jnp.where(cur >= m, NEG_INF, cur)
            tau = m
            cnt = jnp.zeros_like(s1)
            for c in range(PEER_TOPK):
                cnt = jnp.where(s1 + b[c:c + 1, :] >= tau, float(c + 1), cnt)
            cnt_ref[hd, :, cols] = cnt
            e1_ref[hd, :, cols] = jnp.exp(s1 - a[0:1, :]) * (1.0 / zsum)
            rank_ref[hd, :, cols] = pltpu.bitcast(rank.astype(BF16), jnp.uint32)
            e2_ref[hd, :, cols] = pltpu.bitcast(jnp.exp(s2 - b[0:1, :]).astype(BF16), jnp.uint32)
        return carry

    lax.fori_loop(0, PEER_HEADS, head_body, 0)


def peer_route(x2, g, w_pq, sub_keys):
    n, d = x2.shape
    tr = PEER_TR
    qd = w_pq.shape[1]
    f32_shape = jax.ShapeDtypeStruct((PEER_HEADS, PEER_KEYS, n), F32)
    u32_shape = jax.ShapeDtypeStruct((PEER_HEADS, PEER_KEYS // 2, n), jnp.uint32)
    f32_spec = pl.BlockSpec((PEER_HEADS, PEER_KEYS, tr), lambda i: (0, 0, i))
    u32_spec = pl.BlockSpec((PEER_HEADS, PEER_KEYS // 2, tr), lambda i: (0, 0, i))
    return pl.pallas_call(
        _peer_route_kernel,
        grid=(n // tr,),
        in_specs=[
            pl.BlockSpec((tr, d), lambda i: (i, 0)),
            pl.BlockSpec((1, d), lambda i: (0, 0)),
            pl.BlockSpec((d, qd), lambda i: (0, 0)),
            pl.BlockSpec((2, PEER_KEYS, PEER_KEYS), lambda i: (0, 0, 0)),
        ],
        out_specs=[pl.BlockSpec((tr, d), lambda i: (i, 0)),
                   f32_spec, f32_spec, u32_spec, u32_spec],
        out_shape=[jax.ShapeDtypeStruct((n, d), BF16),
                   f32_shape, f32_shape, u32_shape, u32_shape],
        scratch_shapes=[
            pltpu.VMEM((2 * PEER_HEADS, tr, PEER_KEYS), BF16),
            pltpu.VMEM((PEER_TOPK, LANES), F32),
            pltpu.VMEM((PEER_TOPK, LANES), F32),
        ],
        compiler_params=_cparams(1),
        name="peer_route",
    )(x2, g, w_pq, sub_keys)


PEER_TN = 1024
PEER_EB = 1024
PEER_TC = 256


def _gelu_tanh(a):
    inner = math.sqrt(2.0 / math.pi) * (a + 0.044715 * (a * a * a))
    return 0.5 * a * (1.0 + jnp.tanh(inner))


BF16_ROWS = 16


def _peer_dense_kernel(h_ref, x_ref, u_ref, vt_ref, cnt_ref, e1_ref, rank_ref, e2_ref, gf_ref,
                       o_ref, at_scr, g_scr, acc_scr):
    e = pl.program_id(1)
    last = pl.num_programs(1) - 1
    tn = h_ref.shape[0]
    reps = PEER_KEYS // BF16_ROWS

    def row_bf16(ref, hd, ii, cols):
        r16 = jnp.broadcast_to(ref[hd, ii:ii + 1, cols], (BF16_ROWS, LANES)).astype(BF16)
        return jnp.concatenate([r16] * reps, axis=0)

    def scores(c):
        toks = slice(c * PEER_TC, (c + 1) * PEER_TC)
        at_scr[:, toks] = lax.dot_general(u_ref[...], h_ref[toks, :], (((1,), (1,)), ((), ())),
                                          preferred_element_type=F32)

    def gated(c):
        for ii in range(PEER_EB // PEER_KEYS):
            rows = slice(ii * PEER_KEYS, (ii + 1) * PEER_KEYS)
            for lc in range(c * PEER_TC // LANES, (c + 1) * PEER_TC // LANES):
                cols = slice(lc * LANES, (lc + 1) * LANES)
                wgt = jnp.zeros((PEER_KEYS, LANES), BF16)
                for hd in range(PEER_HEADS):
                    rank = pltpu.bitcast(rank_ref[hd, :, cols], BF16)
                    e2 = pltpu.bitcast(e2_ref[hd, :, cols], BF16)
                    kept = jnp.where(rank < row_bf16(cnt_ref, hd, ii, cols), e2,
                                     jnp.zeros((), BF16))
                    wgt = wgt + kept * row_bf16(e1_ref, hd, ii, cols)
                g_scr[rows, cols] = _gelu_tanh(at_scr[rows, cols].astype(BF16)) * wgt

    def combine(c):
        toks = slice(c * PEER_TC, (c + 1) * PEER_TC)
        out_t = jnp.dot(vt_ref[...], g_scr[:, toks], preferred_element_type=F32)
        acc_scr[:, toks] = jnp.where(e == 0, out_t, acc_scr[:, toks] + out_t)

    nch = tn // PEER_TC
    scores(0)
    for c in range(nch):
        if c + 1 < nch:
            scores(c + 1)
        gated(c)
        combine(c)

    @pl.when(e == last)
    def _():
        x3 = x_ref[...] + acc_scr[...].T
        o_ref[...] = _rms(x3, gf_ref[...])


def peer_dense(h3, x2, u, vt, cnt, e1, rank, e2, g_final):
    n, d = x2.shape
    ne = u.shape[0]
    tn, eb = PEER_TN, PEER_EB
    ib = eb // PEER_KEYS
    tok_spec = pl.BlockSpec((PEER_HEADS, PEER_KEYS // 2, tn), lambda i, e: (0, 0, i))
    blk_spec = pl.BlockSpec((PEER_HEADS, ib, tn), lambda i, e: (0, e, i))
    return pl.pallas_call(
        _peer_dense_kernel,
        grid=(n // tn, ne // eb),
        in_specs=[
            pl.BlockSpec((tn, d), lambda i, e: (i, 0)),
            pl.BlockSpec((tn, d), lambda i, e: (i, 0)),
            pl.BlockSpec((eb, d), lambda i, e: (e, 0)),
            pl.BlockSpec((d, eb), lambda i, e: (0, e)),
            blk_spec, blk_spec, tok_spec, tok_spec,
            pl.BlockSpec((1, d), lambda i, e: (0, 0)),
        ],
        out_specs=pl.BlockSpec((tn, d), lambda i, e: (i, 0)),
        out_shape=jax.ShapeDtypeStruct((n, d), F32),
        scratch_shapes=[
            pltpu.VMEM((eb, tn), F32),
            pltpu.VMEM((eb, tn), BF16),
            pltpu.VMEM((d, tn), F32),
        ],
        compiler_params=_cparams(2),
        name="peer_dense",
    )(h3, x2, u, vt, cnt, e1, rank, e2, g_final)


def kernel(x, mem, norm_mix_g, w_in, b_gate, conv_dw_w, conv_dw_b, conv_ln_g, conv_ln_b,
           w_conv_out, w_attn_out, w_out, norm_xattn_g, norm_mem_g, w_xq, w_xkv, w_xo,
           norm_ffn_g, w_peer_q, peer_sub_keys, peer_u, peer_v, norm_final_g):
    batch, seq, d = x.shape
    mem_len = mem.shape[1]
    depth = w_in.shape[0]
    c = conv_dw_w.shape[2]
    d_att = w_attn_out.shape[1]
    n = batch * seq
    xf = x.reshape(n, d)
    memf = mem.reshape(batch * mem_len, d)
    row = lambda v: v.reshape(1, -1).astype(F32)
    q_col = 2 * c // SB_GW
    k_col = q_col + d_att // SB_GW
    v_col = k_col + d_att // SB_GW
    gate_col = (2 * c + 3 * d_att) // d

    for l in range(depth):
        proj = norm_matmul(xf, row(norm_mix_g[l]), w_in[l].astype(BF16), 512, w_in.shape[2] // 2)
        y = dwconv(proj, conv_dw_w[l], row(conv_dw_b[l]), batch, seq, 1024)
        gc = conv_out(y, proj, row(conv_ln_g[l]), row(conv_ln_b[l]), row(b_gate[l, :d]),
                      w_conv_out[l].astype(BF16), gate_col, 512)
        attn = stick_breaking(proj, batch, seq, q_col, k_col, v_col)
        x1 = merge(attn, gc, proj, xf, row(b_gate[l, d:]), w_attn_out[l].astype(BF16),
                   w_out[l].astype(BF16), gate_col + 1, 512)
        kv = norm_matmul(memf, row(norm_mem_g[l]), w_xkv[l].astype(BF16), 256, 1024)
        x2 = cross_attention(x1, row(norm_xattn_g[l]), kv, w_xq[l].astype(BF16),
                             w_xo[l].astype(BF16), seq, mem_len, 512)
        h3, cnt, e1, rank, e2 = peer_route(x2, row(norm_ffn_g[l]), w_peer_q[l].astype(BF16),
                                           peer_sub_keys[l].astype(BF16))
        last = l == depth - 1
        gf = row(norm_final_g) if last else None
        assert last, "the final RMSNorm is fused into the last layer's PEER stage"
        xf = peer_dense(h3, x2, peer_u[l].astype(BF16), peer_v[l].astype(BF16).T,
                        cnt, e1, rank, e2, gf)
    return xf.reshape(batch, seq, d)
```

```python
import functools
import math

import jax
import jax.numpy as jnp
from jax import lax
from jax.experimental import pallas as pl
from jax.experimental.pallas import tpu as pltpu

F32 = jnp.float32
BF16 = jnp.bfloat16

D_MODEL = 1024
CONV_WIDTH = 31
SB_HEADS = 16
SB_HEAD_DIM = 64
XA_HEADS = 4
XA_HEAD_DIM = 256
PEER_HEADS = 8
PEER_KEYS = 128
PEER_TOPK = 16
EPS = 1e-6

LANES = 128
VMEM_LIMIT = 56 * 1024 * 1024

NEG_INF = float("-inf")


def _cparams(n_axes):
    return pltpu.CompilerParams(
        dimension_semantics=("arbitrary",) * n_axes, vmem_limit_bytes=VMEM_LIMIT)


def _rms(x, g):
    ms = jnp.mean(x * x, axis=-1, keepdims=True)
    return x * lax.rsqrt(ms + EPS) * g


def _norm_mm_kernel(x_ref, g_ref, w_ref, o_ref):
    h = _rms(x_ref[...], g_ref[...]).astype(BF16)
    o_ref[...] = jnp.dot(h, w_ref[...], preferred_element_type=F32).astype(o_ref.dtype)


def norm_matmul(x, g, w, tm, tn):
    n, d = x.shape
    m = w.shape[1]
    return pl.pallas_call(
        _norm_mm_kernel,
        grid=(m // tn, n // tm),
        in_specs=[
            pl.BlockSpec((tm, d), lambda j, i: (i, 0)),
            pl.BlockSpec((1, d), lambda j, i: (0, 0)),
            pl.BlockSpec((d, tn), lambda j, i: (0, j)),
        ],
        out_specs=pl.BlockSpec((tm, tn), lambda j, i: (i, j)),
        out_shape=jax.ShapeDtypeStruct((n, m), BF16),
        compiler_params=_cparams(2),
        name="norm_matmul",
    )(x, g, w)


CONV_HALO = 32
CONV_ROWS = 64
CONV_LANES = 128


def _dwconv_kernel(a_ref, gt_ref, ah_ref, gh_ref, w_ref, b_ref, y_ref, u_scr):
    ts = a_ref.shape[0]
    first = pl.program_id(1) == 0
    u_scr[CONV_HALO:CONV_HALO + ts, :] = (
        a_ref[...].astype(F32) * jax.nn.sigmoid(gt_ref[...].astype(F32)))
    uh = ah_ref[...].astype(F32) * jax.nn.sigmoid(gh_ref[...].astype(F32))
    u_scr[0:CONV_HALO, :] = jnp.where(first, 0.0, uh)
    off = CONV_HALO - (CONV_WIDTH - 1)
    for lt in range(CONV_LANES // LANES):
        cols = slice(lt * LANES, (lt + 1) * LANES)
        for r in range(ts // CONV_ROWS):
            r0 = r * CONV_ROWS
            acc = jnp.zeros((CONV_ROWS, LANES), F32) + b_ref[:, cols]
            for k in range(CONV_WIDTH):
                acc = acc + w_ref[k:k + 1, cols] * u_scr[r0 + off + k:r0 + off + k + CONV_ROWS, cols]
            y_ref[r0:r0 + CONV_ROWS, cols] = acc


def dwconv(proj, dw_w, dw_b, batch, seq, ts):
    n = proj.shape[0]
    c = dw_w.shape[1]
    nct = c // CONV_LANES
    spt = seq // ts
    hb = ts // CONV_HALO

    def halo_idx(col0):
        def f(b, i, cc):
            return (jnp.maximum((b * spt + i) * hb - 1, 0), col0 + cc)
        return f

    return pl.pallas_call(
        _dwconv_kernel,
        grid=(batch, spt, nct),
        in_specs=[
            pl.BlockSpec((ts, CONV_LANES), lambda b, i, cc: (b * spt + i, cc)),
            pl.BlockSpec((ts, CONV_LANES), lambda b, i, cc: (b * spt + i, nct + cc)),
            pl.BlockSpec((CONV_HALO, CONV_LANES), halo_idx(0)),
            pl.BlockSpec((CONV_HALO, CONV_LANES), halo_idx(nct)),
            pl.BlockSpec((CONV_WIDTH, CONV_LANES), lambda b, i, cc: (0, cc)),
            pl.BlockSpec((1, CONV_LANES), lambda b, i, cc: (0, cc)),
        ],
        out_specs=pl.BlockSpec((ts, CONV_LANES), lambda b, i, cc: (b * spt + i, cc)),
        out_shape=jax.ShapeDtypeStruct((n, c), F32),
        scratch_shapes=[pltpu.VMEM((CONV_HALO + ts, CONV_LANES), F32)],
        compiler_params=_cparams(3),
        name="dwconv",
    )(proj, proj, proj, proj, dw_w, dw_b)


def _conv_out_kernel(y_ref, gl_ref, lng_ref, lnb_ref, bg_ref, w_ref, o_ref):
    y = y_ref[...]
    mu = jnp.mean(y, axis=-1, keepdims=True)
    yc = y - mu
    var = jnp.mean(yc * yc, axis=-1, keepdims=True)
    yn = yc * lax.rsqrt(var + EPS) * lng_ref[...] + lnb_ref[...]
    act = (yn * jax.nn.sigmoid(yn)).astype(BF16)
    co = jnp.dot(act, w_ref[...], preferred_element_type=F32)
    gate = jax.nn.sigmoid(gl_ref[...].astype(F32) + bg_ref[...])
    o_ref[...] = (gate * co).astype(o_ref.dtype)


def conv_out(y, proj, ln_g, ln_b, bg_conv, w, gate_col_block, tm):
    n, c = y.shape
    d = w.shape[1]
    return pl.pallas_call(
        _conv_out_kernel,
        grid=(n // tm,),
        in_specs=[
            pl.BlockSpec((tm, c), lambda i: (i, 0)),
            pl.BlockSpec((tm, d), lambda i: (i, gate_col_block)),
            pl.BlockSpec((1, c), lambda i: (0, 0)),
            pl.BlockSpec((1, c), lambda i: (0, 0)),
            pl.BlockSpec((1, d), lambda i: (0, 0)),
            pl.BlockSpec((c, d), lambda i: (0, 0)),
        ],
        out_specs=pl.BlockSpec((tm, d), lambda i: (i, 0)),
        out_shape=jax.ShapeDtypeStruct((n, d), BF16),
        compiler_params=_cparams(1),
        name="conv_out",
    )(y, proj, ln_g, ln_b, bg_conv, w)


SB_TQ = 256
SB_TK = 256
SB_CHUNK = 32
SB_GROUP = 4
SB_GW = SB_GROUP * SB_HEAD_DIM


def _softplus(z):
    return jnp.maximum(z, 0.0) + jnp.log(1.0 + jnp.exp2(jnp.abs(z) * (-1.0 / math.log(2.0))))


def _sb_kernel(q_ref, k_ref, v_ref, tri_ref, o_ref, z_scr, sp_scr, w_scr, acc_scr, run_scr):
    qi = pl.program_id(2)
    tq, tk = SB_TQ, SB_TK
    heads = range(SB_GROUP)
    lane = lax.broadcasted_iota(jnp.int32, (1, SB_GW), 1)
    in_head = [(lane >= hd * SB_HEAD_DIM) & (lane < (hd + 1) * SB_HEAD_DIM) for hd in heads]
    qg = q_ref[...]
    scale = jnp.asarray(1.0 / math.sqrt(SB_HEAD_DIM), qg.dtype)
    qh = [jnp.where(in_head[hd], qg, jnp.zeros_like(qg)) * scale for hd in heads]
    chunks = [slice(c * SB_CHUNK, (c + 1) * SB_CHUNK) for c in range(tq // SB_CHUNK)]

    def causal(c):
        qpos = c * SB_CHUNK + lax.broadcasted_iota(jnp.int32, (SB_CHUNK, tk), 0)
        kpos = lax.broadcasted_iota(jnp.int32, (SB_CHUNK, tk), 1)
        return kpos < qpos

    def region(js, diags):
        slots = range(len(js))
        kbs = [k_ref[pl.ds(pl.multiple_of(j * tk, tk), tk), :] for j in js]
        vbs = [v_ref[pl.ds(pl.multiple_of(j * tk, tk), tk), :] for j in js]
        for s in slots:
            for hd in heads:
                z_scr[hd, s] = lax.dot_general(qh[hd], kbs[s], (((1,), (1,)), ((), ())),
                                               preferred_element_type=F32)
        for s in slots:
            for hd in heads:
                for c, rows in enumerate(chunks):
                    sp = _softplus(z_scr[hd, s, rows, :])
                    if diags[s]:
                        sp = jnp.where(causal(c), sp, 0.0)
                    sp_scr[hd, s, rows, :] = sp.astype(BF16)
        for s in slots:
            for hd in heads:
                inc = jnp.dot(sp_scr[hd, s], tri_ref[...], preferred_element_type=F32)
                for c, rows in enumerate(chunks):
                    logw = z_scr[hd, s, rows, :] + inc[rows, :]
                    if diags[s]:
                        w = jnp.where(causal(c), jnp.exp(logw), 0.0)
                        run_scr[hd, rows, :] = inc[rows, 0:1]
                    else:
                        run = run_scr[hd, rows, :]
                        w = jnp.exp(logw + run)
                        run_scr[hd, rows, :] = run + inc[rows, 0:1]
                    w_scr[hd, s, rows, :] = w.astype(BF16)
        for hd in heads:
            for s in slots:
                pv = jnp.dot(w_scr[hd, s], vbs[s], preferred_element_type=F32)
                if diags[s]:
                    acc_scr[hd] = pv
                else:
                    acc_scr[hd] += pv

    odd = qi % 2

    @pl.when(odd == 0)
    def _():
        region([qi], [True])

    @pl.when(odd == 1)
    def _():
        region([qi, qi - 1], [True, False])

    def body(p, carry):
        j = qi - odd - 1 - 2 * p
        region([j, j - 1], [False, False])
        return carry

    lax.fori_loop(0, (qi - odd) // 2, body, 0)
    out = acc_scr[0]
    for hd in range(1, SB_GROUP):
        out = jnp.where(in_head[hd], acc_scr[hd], out)
    o_ref[...] = out.astype(o_ref.dtype)


def _neg_tri(tk):
    j = lax.broadcasted_iota(jnp.int32, (tk, tk), 0)
    s = lax.broadcasted_iota(jnp.int32, (tk, tk), 1)
    return jnp.where(j >= s, -1.0, 0.0).astype(BF16)


def stick_breaking(proj, batch, seq, q_col, k_col, v_col):
    n = proj.shape[0]
    nq = seq // SB_TQ
    ngroup = SB_HEADS // SB_GROUP
    tri = _neg_tri(SB_TK)
    return pl.pallas_call(
        _sb_kernel,
        grid=(batch, ngroup, nq),
        in_specs=[
            pl.BlockSpec((SB_TQ, SB_GW), lambda b, g, qi: (b * nq + qi, q_col + g)),
            pl.BlockSpec((seq, SB_GW), lambda b, g, qi: (b, k_col + g)),
            pl.BlockSpec((seq, SB_GW), lambda b, g, qi: (b, v_col + g)),
            pl.BlockSpec((SB_TK, SB_TK), lambda b, g, qi: (0, 0)),
        ],
        out_specs=pl.BlockSpec((SB_TQ, SB_GW), lambda b, g, qi: (b * nq + qi, g)),
        out_shape=jax.ShapeDtypeStruct((n, SB_HEADS * SB_HEAD_DIM), BF16),
        scratch_shapes=[
            pltpu.VMEM((SB_GROUP, 2, SB_TQ, SB_TK), F32),
            pltpu.VMEM((SB_GROUP, 2, SB_TQ, SB_TK), BF16),
            pltpu.VMEM((SB_GROUP, 2, SB_TQ, SB_TK), BF16),
            pltpu.VMEM((SB_GROUP, SB_TQ, SB_GW), F32),
            pltpu.VMEM((SB_GROUP, SB_TQ, 1), F32),
        ],
        compiler_params=_cparams(3),
        name="stick_breaking",
    )(proj, proj, proj, tri)


def _merge_kernel(attn_ref, gc_ref, gl_ref, x_ref, bg_ref, wa_ref, wo_ref, o_ref):
    ao = jnp.dot(attn_ref[...], wa_ref[...], preferred_element_type=F32)
    gate = jax.nn.sigmoid(gl_ref[...].astype(F32) + bg_ref[...])
    mixed = (gc_ref[...].astype(F32) + gate * ao).astype(BF16)
    o_ref[...] = x_ref[...] + jnp.dot(mixed, wo_ref[...], preferred_element_type=F32)


def merge(attn, gc, proj, x, bg_attn, w_attn_out, w_out, gate_col_block, tm):
    n, d = x.shape
    da = attn.shape[1]
    return pl.pallas_call(
        _merge_kernel,
        grid=(n // tm,),
        in_specs=[
            pl.BlockSpec((tm, da), lambda i: (i, 0)),
            pl.BlockSpec((tm, d), lambda i: (i, 0)),
            pl.BlockSpec((tm, d), lambda i: (i, gate_col_block)),
            pl.BlockSpec((tm, d), lambda i: (i, 0)),
            pl.BlockSpec((1, d), lambda i: (0, 0)),
            pl.BlockSpec((da, d), lambda i: (0, 0)),
            pl.BlockSpec((d, d), lambda i: (0, 0)),
        ],
        out_specs=pl.BlockSpec((tm, d), lambda i: (i, 0)),
        out_shape=jax.ShapeDtypeStruct((n, d), F32),
        compiler_params=_cparams(1),
        name="merge",
    )(attn, gc, proj, x, bg_attn, w_attn_out, w_out)


def _xattn_kernel(x_ref, g_ref, kv_ref, wq_ref, wo_ref, o_ref):
    x = x_ref[...]
    h = _rms(x, g_ref[...]).astype(BF16)
    q = jnp.dot(h, wq_ref[...], preferred_element_type=F32)
    q = (q * (1.0 / math.sqrt(XA_HEAD_DIM))).astype(BF16)
    d = XA_HEADS * XA_HEAD_DIM
    outs = []
    for hd in range(XA_HEADS):
        cols = slice(hd * XA_HEAD_DIM, (hd + 1) * XA_HEAD_DIM)
        kh = kv_ref[:, cols]
        vh = kv_ref[:, d + hd * XA_HEAD_DIM:d + (hd + 1) * XA_HEAD_DIM]
        sc = lax.dot_general(q[:, cols], kh, (((1,), (1,)), ((), ())),
                             preferred_element_type=F32)
        p = jnp.exp(sc - jnp.max(sc, axis=-1, keepdims=True))
        p = (p / jnp.sum(p, axis=-1, keepdims=True)).astype(BF16)
        outs.append(jnp.dot(p, vh, preferred_element_type=F32).astype(BF16))
    o = jnp.concatenate(outs, axis=1)
    o_ref[...] = x + jnp.dot(o, wo_ref[...], preferred_element_type=F32)


def cross_attention(x1, g, kv, w_xq, w_xo, seq, mem_len, tm):
    n, d = x1.shape
    tiles_per_batch = seq // tm
    return pl.pallas_call(
        _xattn_kernel,
        grid=(n // tm,),
        in_specs=[
            pl.BlockSpec((tm, d), lambda i: (i, 0)),
            pl.BlockSpec((1, d), lambda i: (0, 0)),
            pl.BlockSpec((mem_len, 2 * d), lambda i: (i // tiles_per_batch, 0)),
            pl.BlockSpec((d, d), lambda i: (0, 0)),
            pl.BlockSpec((d, d), lambda i: (0, 0)),
        ],
        out_specs=pl.BlockSpec((tm, d), lambda i: (i, 0)),
        out_shape=jax.ShapeDtypeStruct((n, d), F32),
        compiler_params=_cparams(1),
        name="cross_attention",
    )(x1, g, kv, w_xq, w_xo)


PEER_TR = 512
RANK_REST = 127.0


def _peer_route_kernel(x_ref, g_ref, wq_ref, sk_ref, h_ref, cnt_ref, e1_ref, rank_ref, e2_ref,
                       q_scr, a_scr, b_scr):
    h3 = _rms(x_ref[...], g_ref[...]).astype(BF16)
    h_ref[...] = h3
    q = jnp.dot(h3, wq_ref[...], preferred_element_type=F32).astype(BF16)
    for c in range(2 * PEER_HEADS):
        q_scr[c] = q[:, c * PEER_KEYS:(c + 1) * PEER_KEYS]

    def head_body(hd, carry):
        dn = (((1,), (1,)), ((), ()))
        s1_all = lax.dot_general(sk_ref[0], q_scr[2 * hd], dn, preferred_element_type=F32)
        s2_all = lax.dot_general(sk_ref[1], q_scr[2 * hd + 1], dn, preferred_element_type=F32)
        for lc in range(PEER_TR // LANES):
            cols = slice(lc * LANES, (lc + 1) * LANES)
            s1 = s1_all[:, cols]
            s2 = s2_all[:, cols]
            cur = s1
            for r in range(PEER_TOPK):
                m = jnp.max(cur, axis=0, keepdims=True)
                a_scr[r:r + 1, :] = m
                cur = jnp.where(cur >= m, NEG_INF, cur)
            cur = s2
            rank = jnp.full_like(s2, RANK_REST)
            for r in range(PEER_TOPK):
                m = jnp.max(cur, axis=0, keepdims=True)
                b_scr[r:r + 1, :] = m
                hit = cur >= m
                rank = jnp.where(hit, float(r), rank)
                cur = jnp.where(hit, NEG_INF, cur)
            a = a_scr[...]
            b = b_scr[...]
            blocks = [a[0:1, :] + b]
            for r in range(1, 8):
                blocks.append(a[r:r + 1, :] + b[0:8, :])
            blocks.append(a[8:16, :] + b[0:1, :])
            cur = jnp.concatenate(blocks, axis=0)
            top = a[0:1, :] + b[0:1, :]
            zsum = jnp.zeros_like(top)
            m = top
            for r in range(PEER_TOPK):
                m = jnp.max(cur, axis=0, keepdims=True)
                zsum = zsum + jnp.exp(m - top)
                cur = jnp.where(cur >= m, NEG_INF, cur)
            tau = m
            cnt = jnp.zeros_like(s1)
            for c in range(PEER_TOPK):
                cnt = jnp.where(s1 + b[c:c + 1, :] >= tau, float(c + 1), cnt)
            cnt_ref[hd, :, cols] = cnt
            e1_ref[hd, :, cols] = jnp.exp(s1 - a[0:1, :]) * (1.0 / zsum)
            rank_ref[hd, :, cols] = pltpu.bitcast(rank.astype(BF16), jnp.uint32)
            e2_ref[hd, :, cols] = pltpu.bitcast(jnp.exp(s2 - b[0:1, :]).astype(BF16), jnp.uint32)
        return carry

    lax.fori_loop(0, PEER_HEADS, head_body, 0)


def peer_route(x2, g, w_pq, sub_keys):
    n, d = x2.shape
    tr = PEER_TR
    qd = w_pq.shape[1]
    f32_shape = jax.ShapeDtypeStruct((PEER_HEADS, PEER_KEYS, n), F32)
    u32_shape = jax.ShapeDtypeStruct((PEER_HEADS, PEER_KEYS // 2, n), jnp.uint32)
    f32_spec = pl.BlockSpec((PEER_HEADS, PEER_KEYS, tr), lambda i: (0, 0, i))
    u32_spec = pl.BlockSpec((PEER_HEADS, PEER_KEYS // 2, tr), lambda i: (0, 0, i))
    return pl.pallas_call(
        _peer_route_kernel,
        grid=(n // tr,),
        in_specs=[
            pl.BlockSpec((tr, d), lambda i: (i, 0)),
            pl.BlockSpec((1, d), lambda i: (0, 0)),
            pl.BlockSpec((d, qd), lambda i: (0, 0)),
            pl.BlockSpec((2, PEER_KEYS, PEER_KEYS), lambda i: (0, 0, 0)),
        ],
        out_specs=[pl.BlockSpec((tr, d), lambda i: (i, 0)),
                   f32_spec, f32_spec, u32_spec, u32_spec],
        out_shape=[jax.ShapeDtypeStruct((n, d), BF16),
                   f32_shape, f32_shape, u32_shape, u32_shape],
        scratch_shapes=[
            pltpu.VMEM((2 * PEER_HEADS, tr, PEER_KEYS), BF16),
            pltpu.VMEM((PEER_TOPK, LANES), F32),
            pltpu.VMEM((PEER_TOPK, LANES), F32),
        ],
        compiler_params=_cparams(1),
        name="peer_route",
    )(x2, g, w_pq, sub_keys)


PEER_TN = 1024
PEER_EB = 1024
PEER_TC = 1024


def _gelu_tanh(a):
    inner = math.sqrt(2.0 / math.pi) * (a + 0.044715 * (a * a * a))
    return 0.5 * a * (1.0 + jnp.tanh(inner))


BF16_ROWS = 16


def _peer_dense_kernel(h_ref, x_ref, u_ref, vt_ref, cnt_ref, e1_ref, rank_ref, e2_ref, gf_ref,
                       o_ref, at_scr, g_scr, acc_scr):
    e = pl.program_id(1)
    last = pl.num_programs(1) - 1
    tn = h_ref.shape[0]
    reps = PEER_KEYS // BF16_ROWS

    def row_bf16(ref, hd, ii, cols):
        r16 = jnp.broadcast_to(ref[hd, ii:ii + 1, cols], (BF16_ROWS, LANES)).astype(BF16)
        return jnp.concatenate([r16] * reps, axis=0)

    def scores(c):
        toks = slice(c * PEER_TC, (c + 1) * PEER_TC)
        at_scr[:, toks] = lax.dot_general(u_ref[...], h_ref[toks, :], (((1,), (1,)), ((), ())),
                                          preferred_element_type=F32)

    def gated(c):
        for ii in range(PEER_EB // PEER_KEYS):
            rows = slice(ii * PEER_KEYS, (ii + 1) * PEER_KEYS)
            for lc in range(c * PEER_TC // LANES, (c + 1) * PEER_TC // LANES):
                cols = slice(lc * LANES, (lc + 1) * LANES)
                wgt = jnp.zeros((PEER_KEYS, LANES), BF16)
                for hd in range(PEER_HEADS):
                    rank = pltpu.bitcast(rank_ref[hd, :, cols], BF16)
                    e2 = pltpu.bitcast(e2_ref[hd, :, cols], BF16)
                    kept = jnp.where(rank < row_bf16(cnt_ref, hd, ii, cols), e2,
                                     jnp.zeros((), BF16))
                    wgt = wgt + kept * row_bf16(e1_ref, hd, ii, cols)
                g_scr[rows, cols] = _gelu_tanh(at_scr[rows, cols].astype(BF16)) * wgt

    def combine(c):
        toks = slice(c * PEER_TC, (c + 1) * PEER_TC)
        out_t = jnp.dot(vt_ref[...], g_scr[:, toks], preferred_element_type=F32)
        acc_scr[:, toks] = jnp.where(e == 0, out_t, acc_scr[:, toks] + out_t)

    nch = tn // PEER_TC
    scores(0)
    for c in range(nch):
        if c + 1 < nch:
            scores(c + 1)
        gated(c)
        combine(c)

    @pl.when(e == last)
    def _():
        x3 = x_ref[...] + acc_scr[...].T
        o_ref[...] = _rms(x3, gf_ref[...])


def peer_dense(h3, x2, u, vt, cnt, e1, rank, e2, g_final):
    n, d = x2.shape
    ne = u.shape[0]
    tn, eb = PEER_TN, PEER_EB
    ib = eb // PEER_KEYS
    tok_spec = pl.BlockSpec((PEER_HEADS, PEER_KEYS // 2, tn), lambda i, e: (0, 0, i))
    blk_spec = pl.BlockSpec((PEER_HEADS, ib, tn), lambda i, e: (0, e, i))
    return pl.pallas_call(
        _peer_dense_kernel,
        grid=(n // tn, ne // eb),
        in_specs=[
            pl.BlockSpec((tn, d), lambda i, e: (i, 0)),
            pl.BlockSpec((tn, d), lambda i, e: (i, 0)),
            pl.BlockSpec((eb, d), lambda i, e: (e, 0)),
            pl.BlockSpec((d, eb), lambda i, e: (0, e)),
            blk_spec, blk_spec, tok_spec, tok_spec,
            pl.BlockSpec((1, d), lambda i, e: (0, 0)),
        ],
        out_specs=pl.BlockSpec((tn, d), lambda i, e: (i, 0)),
        out_shape=jax.ShapeDtypeStruct((n, d), F32),
        scratch_shapes=[
            pltpu.VMEM((eb, tn), F32),
            pltpu.VMEM((eb, tn), BF16),
            pltpu.VMEM((d, tn), F32),
        ],
        compiler_params=_cparams(2),
        name="peer_dense",
    )(h3, x2, u, vt, cnt, e1, rank, e2, g_final)


def kernel(x, mem, norm_mix_g, w_in, b_gate, conv_dw_w, conv_dw_b, conv_ln_g, conv_ln_b,
           w_conv_out, w_attn_out, w_out, norm_xattn_g, norm_mem_g, w_xq, w_xkv, w_xo,
           norm_ffn_g, w_peer_q, peer_sub_keys, peer_u, peer_v, norm_final_g):
    batch, seq, d = x.shape
    mem_len = mem.shape[1]
    depth = w_in.shape[0]
    c = conv_dw_w.shape[2]
    d_att = w_attn_out.shape[1]
    n = batch * seq
    xf = x.reshape(n, d)
    memf = mem.reshape(batch * mem_len, d)
    row = lambda v: v.reshape(1, -1).astype(F32)
    q_col = 2 * c // SB_GW
    k_col = q_col + d_att // SB_GW
    v_col = k_col + d_att // SB_GW
    gate_col = (2 * c + 3 * d_att) // d

    for l in range(depth):
        proj = norm_matmul(xf, row(norm_mix_g[l]), w_in[l].astype(BF16), 512, w_in.shape[2] // 2)
        y = dwconv(proj, conv_dw_w[l], row(conv_dw_b[l]), batch, seq, 1024)
        gc = conv_out(y, proj, row(conv_ln_g[l]), row(conv_ln_b[l]), row(b_gate[l, :d]),
                      w_conv_out[l].astype(BF16), gate_col, 512)
        attn = stick_breaking(proj, batch, seq, q_col, k_col, v_col)
        x1 = merge(attn, gc, proj, xf, row(b_gate[l, d:]), w_attn_out[l].astype(BF16),
                   w_out[l].astype(BF16), gate_col + 1, 512)
        kv = norm_matmul(memf, row(norm_mem_g[l]), w_xkv[l].astype(BF16), 256, 1024)
        x2 = cross_attention(x1, row(norm_xattn_g[l]), kv, w_xq[l].astype(BF16),
                             w_xo[l].astype(BF16), seq, mem_len, 512)
        h3, cnt, e1, rank, e2 = peer_route(x2, row(norm_ffn_g[l]), w_peer_q[l].astype(BF16),
                                           peer_sub_keys[l].astype(BF16))
        last = l == depth - 1
        gf = row(norm_final_g) if last else None
        assert last, "the final RMSNorm is fused into the last layer's PEER stage"
        xf = peer_dense(h3, x2, peer_u[l].astype(BF16), peer_v[l].astype(BF16).T,
                        cnt, e1, rank, e2, gf)
    return xf.reshape(batch, seq, d)
```

```python
import functools
import math

import jax
import jax.numpy as jnp
from jax import lax
from jax.experimental import pallas as pl
from jax.experimental.pallas import tpu as pltpu

F32 = jnp.float32
BF16 = jnp.bfloat16

D_MODEL = 1024
CONV_WIDTH = 31
SB_HEADS = 16
SB_HEAD_DIM = 64
XA_HEADS = 4
XA_HEAD_DIM = 256
PEER_HEADS = 8
PEER_KEYS = 128
PEER_TOPK = 16
EPS = 1e-6

LANES = 128
VMEM_LIMIT = 56 * 1024 * 1024

NEG_INF = float("-inf")


def _cparams(n_axes):
    return pltpu.CompilerParams(
        dimension_semantics=("arbitrary",) * n_axes, vmem_limit_bytes=VMEM_LIMIT)


def _rms(x, g):
    ms = jnp.mean(x * x, axis=-1, keepdims=True)
    return x * lax.rsqrt(ms + EPS) * g


def _norm_mm_kernel(x_ref, g_ref, w_ref, o_ref):
    h = _rms(x_ref[...], g_ref[...]).astype(BF16)
    o_ref[...] = jnp.dot(h, w_ref[...], preferred_element_type=F32).astype(o_ref.dtype)


def norm_matmul(x, g, w, tm, tn):
    n, d = x.shape
    m = w.shape[1]
    return pl.pallas_call(
        _norm_mm_kernel,
        grid=(m // tn, n // tm),
        in_specs=[
            pl.BlockSpec((tm, d), lambda j, i: (i, 0)),
            pl.BlockSpec((1, d), lambda j, i: (0, 0)),
            pl.BlockSpec((d, tn), lambda j, i: (0, j)),
        ],
        out_specs=pl.BlockSpec((tm, tn), lambda j, i: (i, j)),
        out_shape=jax.ShapeDtypeStruct((n, m), BF16),
        compiler_params=_cparams(2),
        name="norm_matmul",
    )(x, g, w)


CONV_HALO = 32
CONV_ROWS = 64
CONV_LANES = 128


def _dwconv_kernel(a_ref, gt_ref, ah_ref, gh_ref, w_ref, b_ref, y_ref, u_scr):
    ts = a_ref.shape[0]
    first = pl.program_id(1) == 0
    u_scr[CONV_HALO:CONV_HALO + ts, :] = (
        a_ref[...].astype(F32) * jax.nn.sigmoid(gt_ref[...].astype(F32)))
    uh = ah_ref[...].astype(F32) * jax.nn.sigmoid(gh_ref[...].astype(F32))
    u_scr[0:CONV_HALO, :] = jnp.where(first, 0.0, uh)
    off = CONV_HALO - (CONV_WIDTH - 1)
    for lt in range(CONV_LANES // LANES):
        cols = slice(lt * LANES, (lt + 1) * LANES)
        for r in range(ts // CONV_ROWS):
            r0 = r * CONV_ROWS
            acc = jnp.zeros((CONV_ROWS, LANES), F32) + b_ref[:, cols]
            for k in range(CONV_WIDTH):
                acc = acc + w_ref[k:k + 1, cols] * u_scr[r0 + off + k:r0 + off + k + CONV_ROWS, cols]
            y_ref[r0:r0 + CONV_ROWS, cols] = acc


def dwconv(proj, dw_w, dw_b, batch, seq, ts):
    n = proj.shape[0]
    c = dw_w.shape[1]
    nct = c // CONV_LANES
    spt = seq // ts
    hb = ts // CONV_HALO

    def halo_idx(col0):
        def f(b, i, cc):
            return (jnp.maximum((b * spt + i) * hb - 1, 0), col0 + cc)
        return f

    return pl.pallas_call(
        _dwconv_kernel,
        grid=(batch, spt, nct),
        in_specs=[
            pl.BlockSpec((ts, CONV_LANES), lambda b, i, cc: (b * spt + i, cc)),
            pl.BlockSpec((ts, CONV_LANES), lambda b, i, cc: (b * spt + i, nct + cc)),
            pl.BlockSpec((CONV_HALO, CONV_LANES), halo_idx(0)),
            pl.BlockSpec((CONV_HALO, CONV_LANES), halo_idx(nct)),
            pl.BlockSpec((CONV_WIDTH, CONV_LANES), lambda b, i, cc: (0, cc)),
            pl.BlockSpec((1, CONV_LANES), lambda b, i, cc: (0, cc)),
        ],
        out_specs=pl.BlockSpec((ts, CONV_LANES), lambda b, i, cc: (b * spt + i, cc)),
        out_shape=jax.ShapeDtypeStruct((n, c), F32),
        scratch_shapes=[pltpu.VMEM((CONV_HALO + ts, CONV_LANES), F32)],
        compiler_params=_cparams(3),
        name="dwconv",
    )(proj, proj, proj, proj, dw_w, dw_b)


def _conv_out_kernel(y_ref, gl_ref, lng_ref, lnb_ref, bg_ref, w_ref, o_ref):
    y = y_ref[...]
    mu = jnp.mean(y, axis=-1, keepdims=True)
    yc = y - mu
    var = jnp.mean(yc * yc, axis=-1, keepdims=True)
    yn = yc * lax.rsqrt(var + EPS) * lng_ref[...] + lnb_ref[...]
    act = (yn * jax.nn.sigmoid(yn)).astype(BF16)
    co = jnp.dot(act, w_ref[...], preferred_element_type=F32)
    gate = jax.nn.sigmoid(gl_ref[...].astype(F32) + bg_ref[...])
    o_ref[...] = (gate * co).astype(o_ref.dtype)


def conv_out(y, proj, ln_g, ln_b, bg_conv, w, gate_col_block, tm):
    n, c = y.shape
    d = w.shape[1]
    return pl.pallas_call(
        _conv_out_kernel,
        grid=(n // tm,),
        in_specs=[
            pl.BlockSpec((tm, c), lambda i: (i, 0)),
            pl.BlockSpec((tm, d), lambda i: (i, gate_col_block)),
            pl.BlockSpec((1, c), lambda i: (0, 0)),
            pl.BlockSpec((1, c), lambda i: (0, 0)),
            pl.BlockSpec((1, d), lambda i: (0, 0)),
            pl.BlockSpec((c, d), lambda i: (0, 0)),
        ],
        out_specs=pl.BlockSpec((tm, d), lambda i: (i, 0)),
        out_shape=jax.ShapeDtypeStruct((n, d), BF16),
        compiler_params=_cparams(1),
        name="conv_out",
    )(y, proj, ln_g, ln_b, bg_conv, w)


SB_TQ = 256
SB_TK = 256
SB_CHUNK = 32
SB_GROUP = 4
SB_GW = SB_GROUP * SB_HEAD_DIM


def _softplus(z):
    return jnp.maximum(z, 0.0) + jnp.log(1.0 + jnp.exp2(jnp.abs(z) * (-1.0 / math.log(2.0))))


def _sb_kernel(q_ref, k_ref, v_ref, tri_ref, o_ref, z_scr, sp_scr, w_scr, acc_scr, run_scr):
    qi = pl.program_id(2)
    tq, tk = SB_TQ, SB_TK
    heads = range(SB_GROUP)
    lane = lax.broadcasted_iota(jnp.int32, (1, SB_GW), 1)
    in_head = [(lane >= hd * SB_HEAD_DIM) & (lane < (hd + 1) * SB_HEAD_DIM) for hd in heads]
    qg = q_ref[...]
    scale = jnp.asarray(1.0 / math.sqrt(SB_HEAD_DIM), qg.dtype)
    qh = [jnp.where(in_head[hd], qg, jnp.zeros_like(qg)) * scale for hd in heads]
    chunks = [slice(c * SB_CHUNK, (c + 1) * SB_CHUNK) for c in range(tq // SB_CHUNK)]

    def causal(c):
        qpos = c * SB_CHUNK + lax.broadcasted_iota(jnp.int32, (SB_CHUNK, tk), 0)
        kpos = lax.broadcasted_iota(jnp.int32, (SB_CHUNK, tk), 1)
        return kpos < qpos

    def region(js, diags):
        slots = range(len(js))
        kbs = [k_ref[pl.ds(pl.multiple_of(j * tk, tk), tk), :] for j in js]
        vbs = [v_ref[pl.ds(pl.multiple_of(j * tk, tk), tk), :] for j in js]
        for s in slots:
            for hd in heads:
                z_scr[hd, s] = lax.dot_general(qh[hd], kbs[s], (((1,), (1,)), ((), ())),
                                               preferred_element_type=F32)
        for s in slots:
            for hd in heads:
                for c, rows in enumerate(chunks):
                    sp = _softplus(z_scr[hd, s, rows, :])
                    if diags[s]:
                        sp = jnp.where(causal(c), sp, 0.0)
                    sp_scr[hd, s, rows, :] = sp.astype(BF16)
        for s in slots:
            for hd in heads:
                inc = jnp.dot(sp_scr[hd, s], tri_ref[...], preferred_element_type=F32)
                for c, rows in enumerate(chunks):
                    logw = z_scr[hd, s, rows, :] + inc[rows, :]
                    if diags[s]:
                        w = jnp.where(causal(c), jnp.exp(logw), 0.0)
                        run_scr[hd, rows, :] = inc[rows, 0:1]
                    else:
                        run = run_scr[hd, rows, :]
                        w = jnp.exp(logw + run)
                        run_scr[hd, rows, :] = run + inc[rows, 0:1]
                    w_scr[hd, s, rows, :] = w.astype(BF16)
        for hd in heads:
            for s in slots:
                pv = jnp.dot(w_scr[hd, s], vbs[s], preferred_element_type=F32)
                if diags[s]:
                    acc_scr[hd] = pv
                else:
                    acc_scr[hd] += pv

    odd = qi % 2

    @pl.when(odd == 0)
    def _():
        region([qi], [True])

    @pl.when(odd == 1)
    def _():
        region([qi, qi - 1], [True, False])

    def body(p, carry):
        j = qi - odd - 1 - 2 * p
        region([j, j - 1], [False, False])
        return carry

    lax.fori_loop(0, (qi - odd) // 2, body, 0)
    out = acc_scr[0]
    for hd in range(1, SB_GROUP):
        out = jnp.where(in_head[hd], acc_scr[hd], out)
    o_ref[...] = out.astype(o_ref.dtype)


def _neg_tri(tk):
    j = lax.broadcasted_iota(jnp.int32, (tk, tk), 0)
    s = lax.broadcasted_iota(jnp.int32, (tk, tk), 1)
    return jnp.where(j >= s, -1.0, 0.0).astype(BF16)


def stick_breaking(proj, batch, seq, q_col, k_col, v_col):
    n = proj.shape[0]
    nq = seq // SB_TQ
    ngroup = SB_HEADS // SB_GROUP
    tri = _neg_tri(SB_TK)
    return pl.pallas_call(
        _sb_kernel,
        grid=(batch, ngroup, nq),
        in_specs=[
            pl.BlockSpec((SB_TQ, SB_GW), lambda b, g, qi: (b * nq + qi, q_col + g)),
            pl.BlockSpec((seq, SB_GW), lambda b, g, qi: (b, k_col + g)),
            pl.BlockSpec((seq, SB_GW), lambda b, g, qi: (b, v_col + g)),
            pl.BlockSpec((SB_TK, SB_TK), lambda b, g, qi: (0, 0)),
        ],
        out_specs=pl.BlockSpec((SB_TQ, SB_GW), lambda b, g, qi: (b * nq + qi, g)),
        out_shape=jax.ShapeDtypeStruct((n, SB_HEADS * SB_HEAD_DIM), BF16),
        scratch_shapes=[
            pltpu.VMEM((SB_GROUP, 2, SB_TQ, SB_TK), F32),
            pltpu.VMEM((SB_GROUP, 2, SB_TQ, SB_TK), BF16),
            pltpu.VMEM((SB_GROUP, 2, SB_TQ, SB_TK), BF16),
            pltpu.VMEM((SB_GROUP, SB_TQ, SB_GW), F32),
            pltpu.VMEM((SB_GROUP, SB_TQ, 1), F32),
        ],
        compiler_params=_cparams(3),
        name="stick_breaking",
    )(proj, proj, proj, tri)


def _merge_kernel(attn_ref, gc_ref, gl_ref, x_ref, bg_ref, wa_ref, wo_ref, o_ref):
    ao = jnp.dot(attn_ref[...], wa_ref[...], preferred_element_type=F32)
    gate = jax.nn.sigmoid(gl_ref[...].astype(F32) + bg_ref[...])
    mixed = (gc_ref[...].astype(F32) + gate * ao).astype(BF16)
    o_ref[...] = x_ref[...] + jnp.dot(mixed, wo_ref[...], preferred_element_type=F32)


def merge(attn, gc, proj, x, bg_attn, w_attn_out, w_out, gate_col_block, tm):
    n, d = x.shape
    da = attn.shape[1]
    return pl.pallas_call(
        _merge_kernel,
        grid=(n // tm,),
        in_specs=[
            pl.BlockSpec((tm, da), lambda i: (i, 0)),
            pl.BlockSpec((tm, d), lambda i: (i, 0)),
            pl.BlockSpec((tm, d), lambda i: (i, gate_col_block)),
            pl.BlockSpec((tm, d), lambda i: (i, 0)),
            pl.BlockSpec((1, d), lambda i: (0, 0)),
            pl.BlockSpec((da, d), lambda i: (0, 0)),
            pl.BlockSpec((d, d), lambda i: (0, 0)),
        ],
        out_specs=pl.BlockSpec((tm, d), lambda i: (i, 0)),
        out_shape=jax.ShapeDtypeStruct((n, d), F32),
        compiler_params=_cparams(1),
        name="merge",
    )(attn, gc, proj, x, bg_attn, w_attn_out, w_out)


def _xattn_kernel(x_ref, g_ref, kv_ref, wq_ref, wo_ref, o_ref):
    x = x_ref[...]
    h = _rms(x, g_ref[...]).astype(BF16)
    q = jnp.dot(h, wq_ref[...], preferred_element_type=F32)
    q = (q * (1.0 / math.sqrt(XA_HEAD_DIM))).astype(BF16)
    d = XA_HEADS * XA_HEAD_DIM
    outs = []
    for hd in range(XA_HEADS):
        cols = slice(hd * XA_HEAD_DIM, (hd + 1) * XA_HEAD_DIM)
        kh = kv_ref[:, cols]
        vh = kv_ref[:, d + hd * XA_HEAD_DIM:d + (hd + 1) * XA_HEAD_DIM]
        sc = lax.dot_general(q[:, cols], kh, (((1,), (1,)), ((), ())),
                             preferred_element_type=F32)
        p = jnp.exp(sc - jnp.max(sc, axis=-1, keepdims=True))
        p = (p / jnp.sum(p, axis=-1, keepdims=True)).astype(BF16)
        outs.append(jnp.dot(p, vh, preferred_element_type=F32).astype(BF16))
    o = jnp.concatenate(outs, axis=1)
    o_ref[...] = x + jnp.dot(o, wo_ref[...], preferred_element_type=F32)


def cross_attention(x1, g, kv, w_xq, w_xo, seq, mem_len, tm):
    n, d = x1.shape
    tiles_per_batch = seq // tm
    return pl.pallas_call(
        _xattn_kernel,
        grid=(n // tm,),
        in_specs=[
            pl.BlockSpec((tm, d), lambda i: (i, 0)),
            pl.BlockSpec((1, d), lambda i: (0, 0)),
            pl.BlockSpec((mem_len, 2 * d), lambda i: (i // tiles_per_batch, 0)),
            pl.BlockSpec((d, d), lambda i: (0, 0)),
            pl.BlockSpec((d, d), lambda i: (0, 0)),
        ],
        out_specs=pl.BlockSpec((tm, d), lambda i: (i, 0)),
        out_shape=jax.ShapeDtypeStruct((n, d), F32),
        compiler_params=_cparams(1),
        name="cross_attention",
    )(x1, g, kv, w_xq, w_xo)


PEER_TR = 1024
RANK_REST = 127.0


def _peer_route_kernel(x_ref, g_ref, wq_ref, sk_ref, h_ref, cnt_ref, e1_ref, rank_ref, e2_ref,
                       q_scr, a_scr, b_scr):
    h3 = _rms(x_ref[...], g_ref[...]).astype(BF16)
    h_ref[...] = h3
    q = jnp.dot(h3, wq_ref[...], preferred_element_type=F32).astype(BF16)
    for c in range(2 * PEER_HEADS):
        q_scr[c] = q[:, c * PEER_KEYS:(c + 1) * PEER_KEYS]

    def head_body(hd, carry):
        dn = (((1,), (1,)), ((), ()))
        s1_all = lax.dot_general(sk_ref[0], q_scr[2 * hd], dn, preferred_element_type=F32)
        s2_all = lax.dot_general(sk_ref[1], q_scr[2 * hd + 1], dn, preferred_element_type=F32)
        for lc in range(PEER_TR // LANES):
            cols = slice(lc * LANES, (lc + 1) * LANES)
            s1 = s1_all[:, cols]
            s2 = s2_all[:, cols]
            cur = s1
            for r in range(PEER_TOPK):
                m = jnp.max(cur, axis=0, keepdims=True)
                a_scr[r:r + 1, :] = m
                cur = jnp.where(cur >= m, NEG_INF, cur)
            cur = s2
            rank = jnp.full_like(s2, RANK_REST)
            for r in range(PEER_TOPK):
                m = jnp.max(cur, axis=0, keepdims=True)
                b_scr[r:r + 1, :] = m
                hit = cur >= m
                rank = jnp.where(hit, float(r), rank)
                cur = jnp.where(hit, NEG_INF, cur)
            a = a_scr[...]
            b = b_scr[...]
            blocks = [a[0:1, :] + b]
            for r in range(1, 8):
                blocks.append(a[r:r + 1, :] + b[0:8, :])
            blocks.append(a[8:16, :] + b[0:1, :])
            cur = jnp.concatenate(blocks, axis=0)
            top = a[0:1, :] + b[0:1, :]
            zsum = jnp.zeros_like(top)
            m = top
            for r in range(PEER_TOPK):
                m = jnp.max(cur, axis=0, keepdims=True)
                zsum = zsum + jnp.exp(m - top)
                cur = jnp.where(cur >= m, NEG_INF, cur)
            tau = m
            cnt = jnp.zeros_like(s1)
            for c in range(PEER_TOPK):
                cnt = jnp.where(s1 + b[c:c + 1, :] >= tau, float(c + 1), cnt)
            cnt_ref[hd, :, cols] = cnt
            e1_ref[hd, :, cols] = jnp.exp(s1 - a[0:1, :]) * (1.0 / zsum)
            rank_ref[hd, :, cols] = pltpu.bitcast(rank.astype(BF16), jnp.uint32)
            e2_ref[hd, :, cols] = pltpu.bitcast(jnp.exp(s2 - b[0:1, :]).astype(BF16), jnp.uint32)
        return carry

    lax.fori_loop(0, PEER_HEADS, head_body, 0)


def peer_route(x2, g, w_pq, sub_keys):
    n, d = x2.shape
    tr = PEER_TR
    qd = w_pq.shape[1]
    f32_shape = jax.ShapeDtypeStruct((PEER_HEADS, PEER_KEYS, n), F32)
    u32_shape = jax.ShapeDtypeStruct((PEER_HEADS, PEER_KEYS // 2, n), jnp.uint32)
    f32_spec = pl.BlockSpec((PEER_HEADS, PEER_KEYS, tr), lambda i: (0, 0, i))
    u32_spec = pl.BlockSpec((PEER_HEADS, PEER_KEYS // 2, tr), lambda i: (0, 0, i))
    return pl.pallas_call(
        _peer_route_kernel,
        grid=(n // tr,),
        in_specs=[
            pl.BlockSpec((tr, d), lambda i: (i, 0)),
            pl.BlockSpec((1, d), lambda i: (0, 0)),
            pl.BlockSpec((d, qd), lambda i: (0, 0)),
            pl.BlockSpec((2, PEER_KEYS, PEER_KEYS), lambda i: (0, 0, 0)),
        ],
        out_specs=[pl.BlockSpec((tr, d), lambda i: (i, 0)),
                   f32_spec, f32_spec, u32_spec, u32_spec],
        out_shape=[jax.ShapeDtypeStruct((n, d), BF16),
                   f32_shape, f32_shape, u32_shape, u32_shape],
        scratch_shapes=[
            pltpu.VMEM((2 * PEER_HEADS, tr, PEER_KEYS), BF16),
            pltpu.VMEM((PEER_TOPK, LANES), F32),
            pltpu.VMEM((PEER_TOPK, LANES), F32),
        ],
        compiler_params=_cparams(1),
        name="peer_route",
    )(x2, g, w_pq, sub_keys)


PEER_TN = 1024
PEER_EB = 1024
PEER_TC = 256


def _gelu_tanh(a):
    inner = math.sqrt(2.0 / math.pi) * (a + 0.044715 * (a * a * a))
    return 0.5 * a * (1.0 + jnp.tanh(inner))


BF16_ROWS = 16


def _peer_dense_kernel(h_ref, x_ref, u_ref, vt_ref, cnt_ref, e1_ref, rank_ref, e2_ref, gf_ref,
                       o_ref, at_scr, g_scr, acc_scr):
    e = pl.program_id(1)
    last = pl.num_programs(1) - 1
    tn = h_ref.shape[0]
    reps = PEER_KEYS // BF16_ROWS

    def row_bf16(ref, hd, ii, cols):
        r16 = jnp.broadcast_to(ref[hd, ii:ii + 1, cols], (BF16_ROWS, LANES)).astype(BF16)
        return jnp.concatenate([r16] * reps, axis=0)

    def scores(c):
        toks = slice(c * PEER_TC, (c + 1) * PEER_TC)
        at_scr[:, toks] = lax.dot_general(u_ref[...], h_ref[toks, :], (((1,), (1,)), ((), ())),
                                          preferred_element_type=F32)

    def gated(c):
        for ii in range(PEER_EB // PEER_KEYS):
            rows = slice(ii * PEER_KEYS, (ii + 1) * PEER_KEYS)
            for lc in range(c * PEER_TC // LANES, (c + 1) * PEER_TC // LANES):
                cols = slice(lc * LANES, (lc + 1) * LANES)
                wgt = jnp.zeros((PEER_KEYS, LANES), BF16)
                for hd in range(PEER_HEADS):
                    rank = pltpu.bitcast(rank_ref[hd, :, cols], BF16)
                    e2 = pltpu.bitcast(e2_ref[hd, :, cols], BF16)
                    kept = jnp.where(rank < row_bf16(cnt_ref, hd, ii, cols), e2,
                                     jnp.zeros((), BF16))
                    wgt = wgt + kept * row_bf16(e1_ref, hd, ii, cols)
                g_scr[rows, cols] = _gelu_tanh(at_scr[rows, cols].astype(BF16)) * wgt

    def combine(c):
        toks = slice(c * PEER_TC, (c + 1) * PEER_TC)
        out_t = jnp.dot(vt_ref[...], g_scr[:, toks], preferred_element_type=F32)
        acc_scr[:, toks] = jnp.where(e == 0, out_t, acc_scr[:, toks] + out_t)

    nch = tn // PEER_TC
    scores(0)
    for c in range(nch):
        if c + 1 < nch:
            scores(c + 1)
        gated(c)
        combine(c)

    @pl.when(e == last)
    def _():
        x3 = x_ref[...] + acc_scr[...].T
        o_ref[...] = _rms(x3, gf_ref[...])


def peer_dense(h3, x2, u, vt, cnt, e1, rank, e2, g_final):
    n, d = x2.shape
    ne = u.shape[0]
    tn, eb = PEER_TN, PEER_EB
    ib = eb // PEER_KEYS
    tok_spec = pl.BlockSpec((PEER_HEADS, PEER_KEYS // 2, tn), lambda i, e: (0, 0, i))
    blk_spec = pl.BlockSpec((PEER_HEADS, ib, tn), lambda i, e: (0, e, i))
    return pl.pallas_call(
        _peer_dense_kernel,
        grid=(n // tn, ne // eb),
        in_specs=[
            pl.BlockSpec((tn, d), lambda i, e: (i, 0)),
            pl.BlockSpec((tn, d), lambda i, e: (i, 0)),
            pl.BlockSpec((eb, d), lambda i, e: (e, 0)),
            pl.BlockSpec((d, eb), lambda i, e: (0, e)),
            blk_spec, blk_spec, tok_spec, tok_spec,
            pl.BlockSpec((1, d), lambda i, e: (0, 0)),
        ],
        out_specs=pl.BlockSpec((tn, d), lambda i, e: (i, 0)),
        out_shape=jax.ShapeDtypeStruct((n, d), F32),
        scratch_shapes=[
            pltpu.VMEM((eb, tn), F32),
            pltpu.VMEM((eb, tn), BF16),
            pltpu.VMEM((d, tn), F32),
        ],
        compiler_params=_cparams(2),
        name="peer_dense",
    )(h3, x2, u, vt, cnt, e1, rank, e2, g_final)


def kernel(x, mem, norm_mix_g, w_in, b_gate, conv_dw_w, conv_dw_b, conv_ln_g, conv_ln_b,
           w_conv_out, w_attn_out, w_out, norm_xattn_g, norm_mem_g, w_xq, w_xkv, w_xo,
           norm_ffn_g, w_peer_q, peer_sub_keys, peer_u, peer_v, norm_final_g):
    batch, seq, d = x.shape
    mem_len = mem.shape[1]
    depth = w_in.shape[0]
    c = conv_dw_w.shape[2]
    d_att = w_attn_out.shape[1]
    n = batch * seq
    xf = x.reshape(n, d)
    memf = mem.reshape(batch * mem_len, d)
    row = lambda v: v.reshape(1, -1).astype(F32)
    q_col = 2 * c // SB_GW
    k_col = q_col + d_att // SB_GW
    v_col = k_col + d_att // SB_GW
    gate_col = (2 * c + 3 * d_att) // d

    for l in range(depth):
        proj = norm_matmul(xf, row(norm_mix_g[l]), w_in[l].astype(BF16), 512, w_in.shape[2] // 2)
        y = dwconv(proj, conv_dw_w[l], row(conv_dw_b[l]), batch, seq, 1024)
        gc = conv_out(y, proj, row(conv_ln_g[l]), row(conv_ln_b[l]), row(b_gate[l, :d]),
                      w_conv_out[l].astype(BF16), gate_col, 1024)
        attn = stick_breaking(proj, batch, seq, q_col, k_col, v_col)
        x1 = merge(attn, gc, proj, xf, row(b_gate[l, d:]), w_attn_out[l].astype(BF16),
                   w_out[l].astype(BF16), gate_col + 1, 1024)
        kv = norm_matmul(memf, row(norm_mem_g[l]), w_xkv[l].astype(BF16), 256, 1024)
        x2 = cross_attention(x1, row(norm_xattn_g[l]), kv, w_xq[l].astype(BF16),
                             w_xo[l].astype(BF16), seq, mem_len, 1024)
        h3, cnt, e1, rank, e2 = peer_route(x2, row(norm_ffn_g[l]), w_peer_q[l].astype(BF16),
                                           peer_sub_keys[l].astype(BF16))
        last = l == depth - 1
        gf = row(norm_final_g) if last else None
        assert last, "the final RMSNorm is fused into the last layer's PEER stage"
        xf = peer_dense(h3, x2, peer_u[l].astype(BF16), peer_v[l].astype(BF16).T,
                        cnt, e1, rank, e2, gf)
    return xf.reshape(batch, seq, d)
```

```python
import functools
import math

import jax
import jax.numpy as jnp
from jax import lax
from jax.experimental import pallas as pl
from jax.experimental.pallas import tpu as pltpu

F32 = jnp.float32
BF16 = jnp.bfloat16

D_MODEL = 1024
CONV_WIDTH = 31
SB_HEADS = 16
SB_HEAD_DIM = 64
XA_HEADS = 4
XA_HEAD_DIM = 256
PEER_HEADS = 8
PEER_KEYS = 128
PEER_TOPK = 16
EPS = 1e-6

LANES = 128
VMEM_LIMIT = 56 * 1024 * 1024

NEG_INF = float("-inf")


def _cparams(n_axes):
    return pltpu.CompilerParams(
        dimension_semantics=("arbitrary",) * n_axes, vmem_limit_bytes=VMEM_LIMIT)


def _rms(x, g):
    ms = jnp.mean(x * x, axis=-1, keepdims=True)
    return x * lax.rsqrt(ms + EPS) * g


def _norm_mm_kernel(x_ref, g_ref, w_ref, o_ref):
    h = _rms(x_ref[...], g_ref[...]).astype(BF16)
    o_ref[...] = jnp.dot(h, w_ref[...], preferred_element_type=F32).astype(o_ref.dtype)


def norm_matmul(x, g, w, tm, tn):
    n, d = x.shape
    m = w.shape[1]
    return pl.pallas_call(
        _norm_mm_kernel,
        grid=(m // tn, n // tm),
        in_specs=[
            pl.BlockSpec((tm, d), lambda j, i: (i, 0)),
            pl.BlockSpec((1, d), lambda j, i: (0, 0)),
            pl.BlockSpec((d, tn), lambda j, i: (0, j)),
        ],
        out_specs=pl.BlockSpec((tm, tn), lambda j, i: (i, j)),
        out_shape=jax.ShapeDtypeStruct((n, m), BF16),
        compiler_params=_cparams(2),
        name="norm_matmul",
    )(x, g, w)


CONV_HALO = 32
CONV_ROWS = 64
CONV_LANES = 128


def _dwconv_kernel(a_ref, gt_ref, ah_ref, gh_ref, w_ref, b_ref, y_ref, u_scr):
    ts = a_ref.shape[0]
    first = pl.program_id(1) == 0
    u_scr[CONV_HALO:CONV_HALO + ts, :] = (
        a_ref[...].astype(F32) * jax.nn.sigmoid(gt_ref[...].astype(F32)))
    uh = ah_ref[...].astype(F32) * jax.nn.sigmoid(gh_ref[...].astype(F32))
    u_scr[0:CONV_HALO, :] = jnp.where(first, 0.0, uh)
    off = CONV_HALO - (CONV_WIDTH - 1)
    for lt in range(CONV_LANES // LANES):
        cols = slice(lt * LANES, (lt + 1) * LANES)
        for r in range(ts // CONV_ROWS):
            r0 = r * CONV_ROWS
            acc = jnp.zeros((CONV_ROWS, LANES), F32) + b_ref[:, cols]
            for k in range(CONV_WIDTH):
                acc = acc + w_ref[k:k + 1, cols] * u_scr[r0 + off + k:r0 + off + k + CONV_ROWS, cols]
            y_ref[r0:r0 + CONV_ROWS, cols] = acc


def dwconv(proj, dw_w, dw_b, batch, seq, ts):
    n = proj.shape[0]
    c = dw_w.shape[1]
    nct = c // CONV_LANES
    spt = seq // ts
    hb = ts // CONV_HALO

    def halo_idx(col0):
        def f(b, i, cc):
            return (jnp.maximum((b * spt + i) * hb - 1, 0), col0 + cc)
        return f

    return pl.pallas_call(
        _dwconv_kernel,
        grid=(batch, spt, nct),
        in_specs=[
            pl.BlockSpec((ts, CONV_LANES), lambda b, i, cc: (b * spt + i, cc)),
            pl.BlockSpec((ts, CONV_LANES), lambda b, i, cc: (b * spt + i, nct + cc)),
            pl.BlockSpec((CONV_HALO, CONV_LANES), halo_idx(0)),
            pl.BlockSpec((CONV_HALO, CONV_LANES), halo_idx(nct)),
            pl.BlockSpec((CONV_WIDTH, CONV_LANES), lambda b, i, cc: (0, cc)),
            pl.BlockSpec((1, CONV_LANES), lambda b, i, cc: (0, cc)),
        ],
        out_specs=pl.BlockSpec((ts, CONV_LANES), lambda b, i, cc: (b * spt + i, cc)),
        out_shape=jax.ShapeDtypeStruct((n, c), F32),
        scratch_shapes=[pltpu.VMEM((CONV_HALO + ts, CONV_LANES), F32)],
        compiler_params=_cparams(3),
        name="dwconv",
    )(proj, proj, proj, proj, dw_w, dw_b)


def _conv_out_kernel(y_ref, gl_ref, lng_ref, lnb_ref, bg_ref, w_ref, o_ref):
    y = y_ref[...]
    mu = jnp.mean(y, axis=-1, keepdims=True)
    yc = y - mu
    var = jnp.mean(yc * yc, axis=-1, keepdims=True)
    yn = yc * lax.rsqrt(var + EPS) * lng_ref[...] + lnb_ref[...]
    act = (yn * jax.nn.sigmoid(yn)).astype(BF16)
    co = jnp.dot(act, w_ref[...], preferred_element_type=F32)
    gate = jax.nn.sigmoid(gl_ref[...].astype(F32) + bg_ref[...])
    o_ref[...] = (gate * co).astype(o_ref.dtype)


def conv_out(y, proj, ln_g, ln_b, bg_conv, w, gate_col_block, tm):
    n, c = y.shape
    d = w.shape[1]
    return pl.pallas_call(
        _conv_out_kernel,
        grid=(n // tm,),
        in_specs=[
            pl.BlockSpec((tm, c), lambda i: (i, 0)),
            pl.BlockSpec((tm, d), lambda i: (i, gate_col_block)),
            pl.BlockSpec((1, c), lambda i: (0, 0)),
            pl.BlockSpec((1, c), lambda i: (0, 0)),
            pl.BlockSpec((1, d), lambda i: (0, 0)),
            pl.BlockSpec((c, d), lambda i: (0, 0)),
        ],
        out_specs=pl.BlockSpec((tm, d), lambda i: (i, 0)),
        out_shape=jax.ShapeDtypeStruct((n, d), BF16),
        compiler_params=_cparams(1),
        name="conv_out",
    )(y, proj, ln_g, ln_b, bg_conv, w)


SB_TQ = 256
SB_TK = 256
SB_CHUNK = 32
SB_GROUP = 4
SB_GW = SB_GROUP * SB_HEAD_DIM


def _softplus(z):
    return jnp.maximum(z, 0.0) + jnp.log(1.0 + jnp.exp2(jnp.abs(z) * (-1.0 / math.log(2.0))))


def _sb_kernel(q_ref, k_ref, v_ref, tri_ref, o_ref, z_scr, sp_scr, w_scr, acc_scr, run_scr):
    qi = pl.program_id(2)
    tq, tk = SB_TQ, SB_TK
    heads = range(SB_GROUP)
    lane = lax.broadcasted_iota(jnp.int32, (1, SB_GW), 1)
    in_head = [(lane >= hd * SB_HEAD_DIM) & (lane < (hd + 1) * SB_HEAD_DIM) for hd in heads]
    qg = q_ref[...]
    scale = jnp.asarray(1.0 / math.sqrt(SB_HEAD_DIM), qg.dtype)
    qh = [jnp.where(in_head[hd], qg, jnp.zeros_like(qg)) * scale for hd in heads]
    chunks = [slice(c * SB_CHUNK, (c + 1) * SB_CHUNK) for c in range(tq // SB_CHUNK)]

    def causal(c):
        qpos = c * SB_CHUNK + lax.broadcasted_iota(jnp.int32, (SB_CHUNK, tk), 0)
        kpos = lax.broadcasted_iota(jnp.int32, (SB_CHUNK, tk), 1)
        return kpos < qpos

    def region(js, diags):
        slots = range(len(js))
        kbs = [k_ref[pl.ds(pl.multiple_of(j * tk, tk), tk), :] for j in js]
        vbs = [v_ref[pl.ds(pl.multiple_of(j * tk, tk), tk), :] for j in js]
        for s in slots:
            for hd in heads:
                z_scr[hd, s] = lax.dot_general(qh[hd], kbs[s], (((1,), (1,)), ((), ())),
                                               preferred_element_type=F32)
        for s in slots:
            for hd in heads:
                for c, rows in enumerate(chunks):
                    sp = _softplus(z_scr[hd, s, rows, :])
                    if diags[s]:
                        sp = jnp.where(causal(c), sp, 0.0)
                    sp_scr[hd, s, rows, :] = sp.astype(BF16)
        for s in slots:
            for hd in heads:
                inc = jnp.dot(sp_scr[hd, s], tri_ref[...], preferred_element_type=F32)
                for c, rows in enumerate(chunks):
                    logw = z_scr[hd, s, rows, :] + inc[rows, :]
                    if diags[s]:
                        w = jnp.where(causal(c), jnp.exp(logw), 0.0)
                        run_scr[hd, rows, :] = inc[rows, 0:1]
                    else:
                        run = run_scr[hd, rows, :]
                        w = jnp.exp(logw + run)
                        run_scr[hd, rows, :] = run + inc[rows, 0:1]
                    w_scr[hd, s, rows, :] = w.astype(BF16)
        for hd in heads:
            for s in slots:
                pv = jnp.dot(w_scr[hd, s], vbs[s], preferred_element_type=F32)
                if diags[s]:
                    acc_scr[hd] = pv
                else:
                    acc_scr[hd] += pv

    odd = qi % 2

    @pl.when(odd == 0)
    def _():
        region([qi], [True])

    @pl.when(odd == 1)
    def _():
        region([qi, qi - 1], [True, False])

    def body(p, carry):
        j = qi - odd - 1 - 2 * p
        region([j, j - 1], [False, False])
        return carry

    lax.fori_loop(0, (qi - odd) // 2, body, 0)
    out = acc_scr[0]
    for hd in range(1, SB_GROUP):
        out = jnp.where(in_head[hd], acc_scr[hd], out)
    o_ref[...] = out.astype(o_ref.dtype)


def _neg_tri(tk):
    j = lax.broadcasted_iota(jnp.int32, (tk, tk), 0)
    s = lax.broadcasted_iota(jnp.int32, (tk, tk), 1)
    return jnp.where(j >= s, -1.0, 0.0).astype(BF16)


def stick_breaking(proj, batch, seq, q_col, k_col, v_col):
    n = proj.shape[0]
    nq = seq // SB_TQ
    ngroup = SB_HEADS // SB_GROUP
    tri = _neg_tri(SB_TK)
    return pl.pallas_call(
        _sb_kernel,
        grid=(batch, ngroup, nq),
        in_specs=[
            pl.BlockSpec((SB_TQ, SB_GW), lambda b, g, qi: (b * nq + qi, q_col + g)),
            pl.BlockSpec((seq, SB_GW), lambda b, g, qi: (b, k_col + g)),
            pl.BlockSpec((seq, SB_GW), lambda b, g, qi: (b, v_col + g)),
            pl.BlockSpec((SB_TK, SB_TK), lambda b, g, qi: (0, 0)),
        ],
        out_specs=pl.BlockSpec((SB_TQ, SB_GW), lambda b, g, qi: (b * nq + qi, g)),
        out_shape=jax.ShapeDtypeStruct((n, SB_HEADS * SB_HEAD_DIM), BF16),
        scratch_shapes=[
            pltpu.VMEM((SB_GROUP, 2, SB_TQ, SB_TK), F32),
            pltpu.VMEM((SB_GROUP, 2, SB_TQ, SB_TK), BF16),
            pltpu.VMEM((SB_GROUP, 2, SB_TQ, SB_TK), BF16),
            pltpu.VMEM((SB_GROUP, SB_TQ, SB_GW), F32),
            pltpu.VMEM((SB_GROUP, SB_TQ, 1), F32),
        ],
        compiler_params=_cparams(3),
        name="stick_breaking",
    )(proj, proj, proj, tri)


def _merge_kernel(attn_ref, gc_ref, gl_ref, x_ref, bg_ref, wa_ref, wo_ref, o_ref):
    ao = jnp.dot(attn_ref[...], wa_ref[...], preferred_element_type=F32)
    gate = jax.nn.sigmoid(gl_ref[...].astype(F32) + bg_ref[...])
    mixed = (gc_ref[...].astype(F32) + gate * ao).astype(BF16)
    o_ref[...] = x_ref[...] + jnp.dot(mixed, wo_ref[...], preferred_element_type=F32)


def _conv_merge_kernel(attn_ref, y_ref, glc_ref, gla_ref, x_ref, lng_ref, lnb_ref, bgc_ref,
                       bga_ref, wc_ref, wa_ref, wo_ref, o_ref):
    y = y_ref[...]
    yc = y - jnp.mean(y, axis=-1, keepdims=True)
    var = jnp.mean(yc * yc, axis=-1, keepdims=True)
    yn = yc * lax.rsqrt(var + EPS) * lng_ref[...] + lnb_ref[...]
    act = (yn * jax.nn.sigmoid(yn)).astype(BF16)
    co = jnp.dot(act, wc_ref[...], preferred_element_type=F32)
    ao = jnp.dot(attn_ref[...], wa_ref[...], preferred_element_type=F32)
    gate_c = jax.nn.sigmoid(glc_ref[...].astype(F32) + bgc_ref[...])
    gate_a = jax.nn.sigmoid(gla_ref[...].astype(F32) + bga_ref[...])
    mixed = (gate_c * co + gate_a * ao).astype(BF16)
    o_ref[...] = x_ref[...] + jnp.dot(mixed, wo_ref[...], preferred_element_type=F32)


def conv_merge(attn, y, proj, x, ln_g, ln_b, bg_conv, bg_attn, w_conv_out, w_attn_out, w_out,
               gate_col_block, tm):
    n, d = x.shape
    da = attn.shape[1]
    c = y.shape[1]
    tile = lambda w, col: pl.BlockSpec((tm, w), lambda i: (i, col))
    const = lambda r, w: pl.BlockSpec((r, w), lambda i: (0, 0))
    return pl.pallas_call(
        _conv_merge_kernel,
        grid=(n // tm,),
        in_specs=[
            tile(da, 0), tile(c, 0), tile(d, gate_col_block), tile(d, gate_col_block + 1),
            tile(d, 0), const(1, c), const(1, c), const(1, d), const(1, d),
            const(c, d), const(da, d), const(d, d),
        ],
        out_specs=tile(d, 0),
        out_shape=jax.ShapeDtypeStruct((n, d), F32),
        compiler_params=_cparams(1),
        name="conv_merge",
    )(attn, y, proj, proj, x, ln_g, ln_b, bg_conv, bg_attn, w_conv_out, w_attn_out, w_out)


def merge(attn, gc, proj, x, bg_attn, w_attn_out, w_out, gate_col_block, tm):
    n, d = x.shape
    da = attn.shape[1]
    return pl.pallas_call(
        _merge_kernel,
        grid=(n // tm,),
        in_specs=[
            pl.BlockSpec((tm, da), lambda i: (i, 0)),
            pl.BlockSpec((tm, d), lambda i: (i, 0)),
            pl.BlockSpec((tm, d), lambda i: (i, gate_col_block)),
            pl.BlockSpec((tm, d), lambda i: (i, 0)),
            pl.BlockSpec((1, d), lambda i: (0, 0)),
            pl.BlockSpec((da, d), lambda i: (0, 0)),
            pl.BlockSpec((d, d), lambda i: (0, 0)),
        ],
        out_specs=pl.BlockSpec((tm, d), lambda i: (i, 0)),
        out_shape=jax.ShapeDtypeStruct((n, d), F32),
        compiler_params=_cparams(1),
        name="merge",
    )(attn, gc, proj, x, bg_attn, w_attn_out, w_out)


def _xattn_kernel(x_ref, g_ref, kv_ref, wq_ref, wo_ref, o_ref):
    x = x_ref[...]
    h = _rms(x, g_ref[...]).astype(BF16)
    q = jnp.dot(h, wq_ref[...], preferred_element_type=F32)
    q = (q * (1.0 / math.sqrt(XA_HEAD_DIM))).astype(BF16)
    d = XA_HEADS * XA_HEAD_DIM
    outs = []
    for hd in range(XA_HEADS):
        cols = slice(hd * XA_HEAD_DIM, (hd + 1) * XA_HEAD_DIM)
        kh = kv_ref[:, cols]
        vh = kv_ref[:, d + hd * XA_HEAD_DIM:d + (hd + 1) * XA_HEAD_DIM]
        sc = lax.dot_general(q[:, cols], kh, (((1,), (1,)), ((), ())),
                             preferred_element_type=F32)
        p = jnp.exp(sc - jnp.max(sc, axis=-1, keepdims=True))
        p = (p / jnp.sum(p, axis=-1, keepdims=True)).astype(BF16)
        outs.append(jnp.dot(p, vh, preferred_element_type=F32).astype(BF16))
    o = jnp.concatenate(outs, axis=1)
    o_ref[...] = x + jnp.dot(o, wo_ref[...], preferred_element_type=F32)


def cross_attention(x1, g, kv, w_xq, w_xo, seq, mem_len, tm):
    n, d = x1.shape
    tiles_per_batch = seq // tm
    return pl.pallas_call(
        _xattn_kernel,
        grid=(n // tm,),
        in_specs=[
            pl.BlockSpec((tm, d), lambda i: (i, 0)),
            pl.BlockSpec((1, d), lambda i: (0, 0)),
            pl.BlockSpec((mem_len, 2 * d), lambda i: (i // tiles_per_batch, 0)),
            pl.BlockSpec((d, d), lambda i: (0, 0)),
            pl.BlockSpec((d, d), lambda i: (0, 0)),
        ],
        out_specs=pl.BlockSpec((tm, d), lambda i: (i, 0)),
        out_shape=jax.ShapeDtypeStruct((n, d), F32),
        compiler_params=_cparams(1),
        name="cross_attention",
    )(x1, g, kv, w_xq, w_xo)


PEER_TR = 1024
RANK_REST = 127.0


def _peer_route_kernel(x_ref, g_ref, wq_ref, sk_ref, h_ref, cnt_ref, e1_ref, rank_ref, e2_ref,
                       q_scr, a_scr, b_scr):
    h3 = _rms(x_ref[...], g_ref[...]).astype(BF16)
    h_ref[...] = h3
    q = jnp.dot(h3, wq_ref[...], preferred_element_type=F32).astype(BF16)
    for c in range(2 * PEER_HEADS):
        q_scr[c] = q[:, c * PEER_KEYS:(c + 1) * PEER_KEYS]

    def head_body(hd, carry):
        dn = (((1,), (1,)), ((), ()))
        s1_all = lax.dot_general(sk_ref[0], q_scr[2 * hd], dn, preferred_element_type=F32)
        s2_all = lax.dot_general(sk_ref[1], q_scr[2 * hd + 1], dn, preferred_element_type=F32)
        for lc in range(PEER_TR // LANES):
            cols = slice(lc * LANES, (lc + 1) * LANES)
            s1 = s1_all[:, cols]
            s2 = s2_all[:, cols]
            cur = s1
            for r in range(PEER_TOPK):
                m = jnp.max(cur, axis=0, keepdims=True)
                a_scr[r:r + 1, :] = m
                cur = jnp.where(cur >= m, NEG_INF, cur)
            cur = s2
            rank = jnp.full_like(s2, RANK_REST)
            for r in range(PEER_TOPK):
                m = jnp.max(cur, axis=0, keepdims=True)
                b_scr[r:r + 1, :] = m
                hit = cur >= m
                rank = jnp.where(hit, float(r), rank)
                cur = jnp.where(hit, NEG_INF, cur)
            a = a_scr[...]
            b = b_scr[...]
            blocks = [a[0:1, :] + b]
            for r in range(1, 8):
                blocks.append(a[r:r + 1, :] + b[0:8, :])
            blocks.append(a[8:16, :] + b[0:1, :])
            cur = jnp.concatenate(blocks, axis=0)
            top = a[0:1, :] + b[0:1, :]
            zsum = jnp.zeros_like(top)
            m = top
            for r in range(PEER_TOPK):
                m = jnp.max(cur, axis=0, keepdims=True)
                zsum = zsum + jnp.exp(m - top)
                cur = jnp.where(cur >= m, NEG_INF, cur)
            tau = m
            cnt = jnp.zeros_like(s1)
            for c in range(PEER_TOPK):
                cnt = jnp.where(s1 + b[c:c + 1, :] >= tau, float(c + 1), cnt)
            cnt_ref[hd, :, cols] = cnt
            e1_ref[hd, :, cols] = jnp.exp(s1 - a[0:1, :]) * (1.0 / zsum)
            rank_ref[hd, :, cols] = pltpu.bitcast(rank.astype(BF16), jnp.uint32)
            e2_ref[hd, :, cols] = pltpu.bitcast(jnp.exp(s2 - b[0:1, :]).astype(BF16), jnp.uint32)
        return carry

    lax.fori_loop(0, PEER_HEADS, head_body, 0)


def peer_route(x2, g, w_pq, sub_keys):
    n, d = x2.shape
    tr = PEER_TR
    qd = w_pq.shape[1]
    f32_shape = jax.ShapeDtypeStruct((PEER_HEADS, PEER_KEYS, n), F32)
    u32_shape = jax.ShapeDtypeStruct((PEER_HEADS, PEER_KEYS // 2, n), jnp.uint32)
    f32_spec = pl.BlockSpec((PEER_HEADS, PEER_KEYS, tr), lambda i: (0, 0, i))
    u32_spec = pl.BlockSpec((PEER_HEADS, PEER_KEYS // 2, tr), lambda i: (0, 0, i))
    return pl.pallas_call(
        _peer_route_kernel,
        grid=(n // tr,),
        in_specs=[
            pl.BlockSpec((tr, d), lambda i: (i, 0)),
            pl.BlockSpec((1, d), lambda i: (0, 0)),
            pl.BlockSpec((d, qd), lambda i: (0, 0)),
            pl.BlockSpec((2, PEER_KEYS, PEER_KEYS), lambda i: (0, 0, 0)),
        ],
        out_specs=[pl.BlockSpec((tr, d), lambda i: (i, 0)),
                   f32_spec, f32_spec, u32_spec, u32_spec],
        out_shape=[jax.ShapeDtypeStruct((n, d), BF16),
                   f32_shape, f32_shape, u32_shape, u32_shape],
        scratch_shapes=[
            pltpu.VMEM((2 * PEER_HEADS, tr, PEER_KEYS), BF16),
            pltpu.VMEM((PEER_TOPK, LANES), F32),
            pltpu.VMEM((PEER_TOPK, LANES), F32),
        ],
        compiler_params=_cparams(1),
        name="peer_route",
    )(x2, g, w_pq, sub_keys)


PEER_TN = 1024
PEER_EB = 1024
PEER_TC = 256


def _gelu_tanh(a):
    inner = math.sqrt(2.0 / math.pi) * (a + 0.044715 * (a * a * a))
    return 0.5 * a * (1.0 + jnp.tanh(inner))


BF16_ROWS = 16


def _peer_dense_kernel(h_ref, x_ref, u_ref, vt_ref, cnt_ref, e1_ref, rank_ref, e2_ref, gf_ref,
                       o_ref, at_scr, g_scr, acc_scr):
    e = pl.program_id(1)
    last = pl.num_programs(1) - 1
    tn = h_ref.shape[0]
    reps = PEER_KEYS // BF16_ROWS

    def row_bf16(ref, hd, ii, cols):
        r16 = jnp.broadcast_to(ref[hd, ii:ii + 1, cols], (BF16_ROWS, LANES)).astype(BF16)
        return jnp.concatenate([r16] * reps, axis=0)

    def scores(c):
        toks = slice(c * PEER_TC, (c + 1) * PEER_TC)
        at_scr[:, toks] = lax.dot_general(u_ref[...], h_ref[toks, :], (((1,), (1,)), ((), ())),
                                          preferred_element_type=F32)

    def gated(c):
        for ii in range(PEER_EB // PEER_KEYS):
            rows = slice(ii * PEER_KEYS, (ii + 1) * PEER_KEYS)
            for lc in range(c * PEER_TC // LANES, (c + 1) * PEER_TC // LANES):
                cols = slice(lc * LANES, (lc + 1) * LANES)
                wgt = jnp.zeros((PEER_KEYS, LANES), BF16)
                for hd in range(PEER_HEADS):
                    rank = pltpu.bitcast(rank_ref[hd, :, cols], BF16)
                    e2 = pltpu.bitcast(e2_ref[hd, :, cols], BF16)
                    kept = jnp.where(rank < row_bf16(cnt_ref, hd, ii, cols), e2,
                                     jnp.zeros((), BF16))
                    wgt = wgt + kept * row_bf16(e1_ref, hd, ii, cols)
                g_scr[rows, cols] = _gelu_tanh(at_scr[rows, cols].astype(BF16)) * wgt

    def combine(c):
        toks = slice(c * PEER_TC, (c + 1) * PEER_TC)
        out_t = jnp.dot(vt_ref[...], g_scr[:, toks], preferred_element_type=F32)
        acc_scr[:, toks] = jnp.where(e == 0, out_t, acc_scr[:, toks] + out_t)

    nch = tn // PEER_TC
    scores(0)
    for c in range(nch):
        if c + 1 < nch:
            scores(c + 1)
        gated(c)
        combine(c)

    @pl.when(e == last)
    def _():
        x3 = x_ref[...] + acc_scr[...].T
        o_ref[...] = _rms(x3, gf_ref[...])


def peer_dense(h3, x2, u, vt, cnt, e1, rank, e2, g_final):
    n, d = x2.shape
    ne = u.shape[0]
    tn, eb = PEER_TN, PEER_EB
    ib = eb // PEER_KEYS
    tok_spec = pl.BlockSpec((PEER_HEADS, PEER_KEYS // 2, tn), lambda i, e: (0, 0, i))
    blk_spec = pl.BlockSpec((PEER_HEADS, ib, tn), lambda i, e: (0, e, i))
    return pl.pallas_call(
        _peer_dense_kernel,
        grid=(n // tn, ne // eb),
        in_specs=[
            pl.BlockSpec((tn, d), lambda i, e: (i, 0)),
            pl.BlockSpec((tn, d), lambda i, e: (i, 0)),
            pl.BlockSpec((eb, d), lambda i, e: (e, 0)),
            pl.BlockSpec((d, eb), lambda i, e: (0, e)),
            blk_spec, blk_spec, tok_spec, tok_spec,
            pl.BlockSpec((1, d), lambda i, e: (0, 0)),
        ],
        out_specs=pl.BlockSpec((tn, d), lambda i, e: (i, 0)),
        out_shape=jax.ShapeDtypeStruct((n, d), F32),
        scratch_shapes=[
            pltpu.VMEM((eb, tn), F32),
            pltpu.VMEM((eb, tn), BF16),
            pltpu.VMEM((d, tn), F32),
        ],
        compiler_params=_cparams(2),
        name="peer_dense",
    )(h3, x2, u, vt, cnt, e1, rank, e2, g_final)


def kernel(x, mem, norm_mix_g, w_in, b_gate, conv_dw_w, conv_dw_b, conv_ln_g, conv_ln_b,
           w_conv_out, w_attn_out, w_out, norm_xattn_g, norm_mem_g, w_xq, w_xkv, w_xo,
           norm_ffn_g, w_peer_q, peer_sub_keys, peer_u, peer_v, norm_final_g):
    batch, seq, d = x.shape
    mem_len = mem.shape[1]
    depth = w_in.shape[0]
    c = conv_dw_w.shape[2]
    d_att = w_attn_out.shape[1]
    n = batch * seq
    xf = x.reshape(n, d)
    memf = mem.reshape(batch * mem_len, d)
    row = lambda v: v.reshape(1, -1).astype(F32)
    q_col = 2 * c // SB_GW
    k_col = q_col + d_att // SB_GW
    v_col = k_col + d_att // SB_GW
    gate_col = (2 * c + 3 * d_att) // d

    for l in range(depth):
        proj = norm_matmul(xf, row(norm_mix_g[l]), w_in[l].astype(BF16), 512, w_in.shape[2] // 2)
        y = dwconv(proj, conv_dw_w[l], row(conv_dw_b[l]), batch, seq, 1024)
        attn = stick_breaking(proj, batch, seq, q_col, k_col, v_col)
        x1 = conv_merge(attn, y, proj, xf, row(conv_ln_g[l]), row(conv_ln_b[l]),
                        row(b_gate[l, :d]), row(b_gate[l, d:]), w_conv_out[l].astype(BF16),
                        w_attn_out[l].astype(BF16), w_out[l].astype(BF16), gate_col, 512)
        kv = norm_matmul(memf, row(norm_mem_g[l]), w_xkv[l].astype(BF16), 256, 1024)
        x2 = cross_attention(x1, row(norm_xattn_g[l]), kv, w_xq[l].astype(BF16),
                             w_xo[l].astype(BF16), seq, mem_len, 512)
        h3, cnt, e1, rank, e2 = peer_route(x2, row(norm_ffn_g[l]), w_peer_q[l].astype(BF16),
                                           peer_sub_keys[l].astype(BF16))
        last = l == depth - 1
        gf = row(norm_final_g) if last else None
        assert last, "the final RMSNorm is fused into the last layer's PEER stage"
        xf = peer_dense(h3, x2, peer_u[l].astype(BF16), peer_v[l].astype(BF16).T,
                        cnt, e1, rank, e2, gf)
    return xf.reshape(batch, seq, d)
```
